```python
import jax
import jax.numpy as jnp
from jax import lax
import numpy as np

D_MODEL = 1024
BATCH = 4
SEQ = 4096
DEPTH = 1
DEC_BATCH = 128
DEC_SEQ = 8
PAST_LEN = 2048
PAGE_SIZE = 128

D_RNN = D_MODEL
LRU_BLOCKS = 16
LRU_BLOCK_W = D_RNN // LRU_BLOCKS
CONV_W = 4
LRU_C = 8.0
N_HEADS = 16
HEAD_DIM = D_MODEL // N_HEADS
N_KV_HEADS = 4
ROT_DIM = HEAD_DIM // 4
ROPE_THETA = 500000.0
IDX_HEADS = 8
IDX_DIM = 64
IDX_ROT_DIM = IDX_DIM // 4
TOPK_KEYS = 256
Q_BLOCK = 128
N_EXPERTS = 256
EXPERT_TOP_K = 8
N_EXPERT_GROUPS = 8
TOPK_GROUPS = 4
D_EXPERT = D_MODEL // 4
D_SHARED = D_EXPERT
ROUTED_SCALE = 2.5
MOE_BLOCK = 128
DN_ALPHA = (2.0 * DEPTH) ** 0.25
DN_BETA = (8.0 * DEPTH) ** -0.25
LN_EPS = 1e-5
IN_SIZES = (D_RNN, D_RNN, N_HEADS * HEAD_DIM, N_KV_HEADS * HEAD_DIM, N_KV_HEADS * HEAD_DIM,
            IDX_HEADS * IDX_DIM, IDX_DIM, IDX_HEADS, D_MODEL, D_MODEL)
D_IN = sum(IN_SIZES)

kernel_name = 'hawk_dsa_moe_deepnorm_step'


def layer_norm(x, g, b):
    xf = x.astype(jnp.float32)
    mu = jnp.mean(xf, axis=-1, keepdims=True)
    var = jnp.mean(jnp.square(xf - mu), axis=-1, keepdims=True)
    y = (xf - mu) * lax.rsqrt(var + LN_EPS) * g.astype(jnp.float32) + b.astype(jnp.float32)
    return y.astype(x.dtype)


def partial_rope(x, pos, rot_dim):
    half = rot_dim // 2
    inv_freq = jnp.power(ROPE_THETA, -jnp.arange(half, dtype=jnp.float32) * (2.0 / rot_dim))
    ang = pos.astype(jnp.float32)[:, None] * inv_freq[None, :]
    cos = jnp.cos(ang)[None, :, None, :].astype(x.dtype)
    sin = jnp.sin(ang)[None, :, None, :].astype(x.dtype)
    x1 = x[..., :half]
    x2 = x[..., half:rot_dim]
    return jnp.concatenate([x1 * cos - x2 * sin, x1 * sin + x2 * cos, x[..., rot_dim:]], axis=-1)


def project_and_position(x, pos, w_in):
    B, T, _ = x.shape
    split_points = np.cumsum(IN_SIZES)[:-1].tolist()
    z = jnp.einsum('btd,de->bte', x, w_in)
    xr, gr, q, k, v, iq, ik, iw, ga, gb = jnp.split(z, split_points, axis=-1)
    q = partial_rope(q.reshape(B, T, N_HEADS, HEAD_DIM), pos, ROT_DIM)
    k = partial_rope(k.reshape(B, T, N_KV_HEADS, HEAD_DIM), pos, ROT_DIM)
    v = v.reshape(B, T, N_KV_HEADS, HEAD_DIM)
    iq = partial_rope(iq.reshape(B, T, IDX_HEADS, IDX_DIM), pos, IDX_ROT_DIM)
    ik = partial_rope(ik.reshape(B, T, 1, IDX_DIM), pos, IDX_ROT_DIM)[:, :, 0]
    iw = iw * (IDX_HEADS ** -0.5 * IDX_DIM ** -0.5)
    return xr, gr, q, k, v, iq, ik, iw, ga, gb


def rglru_branch(x_in, gate_in, conv_state, h0, conv_w, conv_b, w_a, b_a, w_i, b_i, lru_lambda):
    B, T, _ = x_in.shape
    xp = jnp.concatenate([conv_state.astype(x_in.dtype), x_in], axis=1)
    xc = conv_b
    for j in range(CONV_W):
        xc = xc + conv_w[j] * xp[:, j:j + T]
    new_conv = xp[:, T:]
    xb = xc.reshape(B, T, LRU_BLOCKS, LRU_BLOCK_W)
    r = jax.nn.sigmoid(jnp.einsum('btnd,nde->btne', xb, w_a) + b_a).reshape(B, T, D_RNN)
    i = jax.nn.sigmoid(jnp.einsum('btnd,nde->btne', xb, w_i) + b_i).reshape(B, T, D_RNN)
    log_a = -LRU_C * jax.nn.softplus(-lru_lambda.astype(jnp.float32)) * r.astype(jnp.float32)
    a = jnp.exp(log_a)
    u = jnp.sqrt(-jnp.expm1(2.0 * log_a)) * (i * xc).astype(jnp.float32)

    def step(h, au):
        a_t, u_t = au
        h = a_t * h + u_t
        return h, h

    h_last, hs = lax.scan(step, h0.astype(jnp.float32), (jnp.moveaxis(a, 1, 0), jnp.moveaxis(u, 1, 0)))
    y = jnp.moveaxis(hs, 0, 1).astype(x_in.dtype) * jax.nn.gelu(gate_in)
    return y, new_conv, h_last.astype(x_in.dtype)


def index_scores(iq, iw, ik):
    s = jnp.einsum('bthd,bsd->bths', iq.astype(jnp.float32), ik.astype(jnp.float32))
    return jnp.einsum('bths,bth->bts', jax.nn.relu(s), iw.astype(jnp.float32))


def attend_selected(q, k_sel, v_sel, valid):
    B, T, H, Dh = q.shape
    qg = q.reshape(B, T, N_KV_HEADS, H // N_KV_HEADS, Dh).astype(jnp.float32)
    s = jnp.einsum('btgrd,btkgd->btgrk', qg, k_sel.astype(jnp.float32)) * (Dh ** -0.5)
    s = jnp.where(valid[:, :, None, None, :], s, -jnp.inf)
    p = jax.nn.softmax(s, axis=-1)
    o = jnp.einsum('btgrk,btkgd->btgrd', p, v_sel.astype(jnp.float32))
    return o.reshape(B, T, H * Dh).astype(q.dtype)


def prompt_sparse_attention(q, k, v, iq, iw, ik, pos):
    B, T = q.shape[:2]
    n_keep = min(TOPK_KEYS, T // 4)
    nb = T // Q_BLOCK
    take = jax.vmap(lambda rows, ids: rows[ids])

    def blockify(a):
        return jnp.moveaxis(a.reshape((B, nb, Q_BLOCK) + a.shape[2:]), 1, 0)

    def one_block(args):
        q_b, iq_b, iw_b, pos_b = args
        sc = index_scores(iq_b, iw_b, ik)
        sc = jnp.where(pos[None, None, :] <= pos_b[None, :, None], sc, -jnp.inf)
        _, sel = lax.top_k(sc, n_keep)
        valid = pos[sel] <= pos_b[None, :, None]
        return attend_selected(q_b, take(k, sel), take(v, sel), valid)

    out = lax.map(one_block, (blockify(q), blockify(iq), blockify(iw), pos.reshape(nb, Q_BLOCK)))
    return jnp.moveaxis(out, 0, 1).reshape(B, T, -1)


def sample_sparse_attention(q, k_new, v_new, iq, iw, ik_new, pos, cache_k, cache_v, cache_idx_k, page_table):
    B, T = q.shape[:2]
    past = page_table.shape[1] * PAGE_SIZE
    n_keys = past + T
    n_keep = min(TOPK_KEYS, n_keys // 4)
    take = jax.vmap(lambda rows, ids: rows[ids])
    ik_past = cache_idx_k[page_table].reshape(B, past, IDX_DIM)
    ik_all = jnp.concatenate([ik_past, ik_new.astype(ik_past.dtype)], axis=1)
    key_pos = jnp.arange(n_keys, dtype=jnp.int32)
    sc = index_scores(iq, iw, ik_all)
    sc = jnp.where(key_pos[None, None, :] <= pos[None, :, None], sc, -jnp.inf)
    _, sel = lax.top_k(sc, n_keep)
    valid = sel <= pos[None, :, None]
    from_past = (sel < past)[..., None, None]
    sp = jnp.minimum(sel, past - 1)
    phys = page_table[jnp.arange(B)[:, None, None], sp // PAGE_SIZE]
    off = sp % PAGE_SIZE
    sn = jnp.clip(sel - past, 0, T - 1)
    k_sel = jnp.where(from_past, cache_k[phys, off], take(k_new, sn))
    v_sel = jnp.where(from_past, cache_v[phys, off], take(v_new, sn))
    return attend_selected(q, k_sel, v_sel, valid)


def merge_branches(y_rnn, y_att, ga, gb, w_branch, w_out):
    pa = jnp.einsum('bte,ed->btd', y_rnn, w_branch[:D_RNN])
    pb = jnp.einsum('bte,ed->btd', y_att, w_branch[D_RNN:])
    m = jax.nn.sigmoid(ga) * pa + jax.nn.sigmoid(gb) * pb
    return jnp.einsum('btd,de->bte', m, w_out)


def swiglu(x, wg, wu, wd):
    return (jax.nn.silu(x @ wg) * (x @ wu)) @ wd


def route_tokens(x2d, w_router, router_bias):
    n = x2d.shape[0]
    per_group = N_EXPERTS // N_EXPERT_GROUPS
    scores = jax.nn.sigmoid(jnp.einsum('nd,de->ne', x2d.astype(jnp.float32), w_router.astype(jnp.float32)))
    biased = scores + router_bias.astype(jnp.float32)
    group_score = lax.top_k(biased.reshape(n, N_EXPERT_GROUPS, per_group), 2)[0].sum(-1)
    _, top_groups = lax.top_k(group_score, TOPK_GROUPS)
    group_ok = jnp.any(top_groups[:, :, None] == jnp.arange(N_EXPERT_GROUPS)[None, None, :], axis=1)
    expert_ok = jnp.repeat(group_ok, per_group, axis=1)
    _, experts = lax.top_k(jnp.where(expert_ok, biased, -jnp.inf), EXPERT_TOP_K)
    sel = jnp.take_along_axis(scores, experts, axis=1)
    gates = ROUTED_SCALE * sel / jnp.sum(sel, axis=-1, keepdims=True)
    return experts, gates


def routed_experts(x2d, experts, gates, w_exp_gate, w_exp_up, w_exp_down):
    n, d = x2d.shape
    n_assign = n * EXPERT_TOP_K
    n_rows = -(-(n_assign + N_EXPERTS * (MOE_BLOCK - 1)) // MOE_BLOCK) * MOE_BLOCK
    n_blocks = n_rows // MOE_BLOCK
    expert_flat = experts.reshape(n_assign)
    token_flat = jnp.arange(n_assign, dtype=jnp.int32) // EXPERT_TOP_K
    gate_flat = gates.reshape(n_assign)
    order = jnp.argsort(expert_flat)
    e_sorted = expert_flat[order]
    counts = jnp.zeros((N_EXPERTS,), jnp.int32).at[expert_flat].add(1)
    padded = (counts + MOE_BLOCK - 1) // MOE_BLOCK * MOE_BLOCK
    pad_end = jnp.cumsum(padded)
    raw_start = jnp.cumsum(counts) - counts
    dest = (pad_end - padded)[e_sorted] + jnp.arange(n_assign, dtype=jnp.int32) - raw_start[e_sorted]
    row_token = jnp.zeros((n_rows,), jnp.int32).at[dest].set(token_flat[order])
    row_gate = jnp.zeros((n_rows,), x2d.dtype).at[dest].set(gate_flat[order])
    block_start = jnp.arange(n_blocks, dtype=jnp.int32) * MOE_BLOCK
    block_expert = jnp.minimum(jnp.searchsorted(pad_end, block_start, side='right'), N_EXPERTS - 1)

    def run_block(args):
        tok, g, e = args
        xb = x2d[tok]
        hb = jax.nn.silu(xb @ w_exp_gate[e]) * (xb @ w_exp_up[e])
        return (hb @ w_exp_down[e]) * g[:, None]

    out = lax.map(run_block, (row_token.reshape(n_blocks, MOE_BLOCK), row_gate.reshape(n_blocks, MOE_BLOCK), block_expert))
    return jax.ops.segment_sum(out.reshape(n_rows, d), row_token, num_segments=n)


def moe_ffn(x, w_router, router_bias, w_exp_gate, w_exp_up, w_exp_down, w_sh_gate, w_sh_up, w_sh_down):
    B, T, D = x.shape
    x2d = x.reshape(B * T, D)
    experts, gates = route_tokens(x2d, w_router, router_bias)
    y = routed_experts(x2d, experts, gates.astype(x.dtype), w_exp_gate, w_exp_up, w_exp_down)
    y = y + swiglu(x2d, w_sh_gate, w_sh_up, w_sh_down)
    return y.reshape(B, T, D)


def post_norm_layer(x, y_rnn, y_att, ga, gb, w_branch, w_out, ln1_g, ln1_b, moe_params, ln2_g, ln2_b):
    mix = merge_branches(y_rnn, y_att, ga, gb, w_branch, w_out)
    h = layer_norm(DN_ALPHA * x + mix, ln1_g, ln1_b)
    return layer_norm(DN_ALPHA * h + moe_ffn(h, *moe_params), ln2_g, ln2_b)


def decoder_layer(xp, xs, cache_k, cache_v, cache_idx_k, state_conv, state_h, page_table,
                  w_in, conv_w, conv_b, w_a, b_a, w_i, b_i, lru_lambda, w_branch, w_out, ln1_g, ln1_b,
                  w_router, router_bias, w_exp_gate, w_exp_up, w_exp_down, w_sh_gate, w_sh_up, w_sh_down,
                  ln2_g, ln2_b):
    lru_params = (conv_w, conv_b, w_a, b_a, w_i, b_i, lru_lambda)
    moe_params = (w_router, router_bias, w_exp_gate, w_exp_up, w_exp_down, w_sh_gate, w_sh_up, w_sh_down)
    past = page_table.shape[1] * PAGE_SIZE
    pos_p = jnp.arange(xp.shape[1], dtype=jnp.int32)
    pos_s = past + jnp.arange(xs.shape[1], dtype=jnp.int32)

    xr, gr, q, k_p, v_p, iq, ik_p, iw, ga, gb = project_and_position(xp, pos_p, w_in)
    conv0 = jnp.zeros((xp.shape[0], CONV_W - 1, D_RNN), xp.dtype)
    h0 = jnp.zeros((xp.shape[0], D_RNN), jnp.float32)
    y_rnn, conv_p, h_p = rglru_branch(xr, gr, conv0, h0, *lru_params)
    y_att = prompt_sparse_attention(q, k_p, v_p, iq, iw, ik_p, pos_p)
    yp = post_norm_layer(xp, y_rnn, y_att, ga, gb, w_branch, w_out, ln1_g, ln1_b, moe_params, ln2_g, ln2_b)

    xr, gr, q, k_s, v_s, iq, ik_s, iw, ga, gb = project_and_position(xs, pos_s, w_in)
    y_rnn, conv_s, h_s = rglru_branch(xr, gr, state_conv, state_h, *lru_params)
    y_att = sample_sparse_attention(q, k_s, v_s, iq, iw, ik_s, pos_s, cache_k, cache_v, cache_idx_k, page_table)
    ys = post_norm_layer(xs, y_rnn, y_att, ga, gb, w_branch, w_out, ln1_g, ln1_b, moe_params, ln2_g, ln2_b)
    return yp, ys, (k_p, v_p, ik_p, conv_p, h_p, k_s, v_s, ik_s, conv_s, h_s)


def setup_inputs(seed: int = 0) -> dict:
    key = jax.random.key(seed)
    ks = jax.random.split(key, 32)
    f32 = jnp.float32

    def nrm(k, shape, scale):
        return jax.random.normal(k, shape, f32) * scale

    n_pages = PAST_LEN // PAGE_SIZE
    n_pool = (DEC_BATCH * n_pages * 5) // 4
    L = DEPTH
    x_prompt = nrm(ks[0], (BATCH, SEQ, D_MODEL), 1.0)
    x_sample = nrm(ks[1], (DEC_BATCH, DEC_SEQ, D_MODEL), 1.0)
    cache_k = nrm(ks[2], (L, n_pool, PAGE_SIZE, N_KV_HEADS, HEAD_DIM), 1.0)
    cache_v = nrm(ks[3], (L, n_pool, PAGE_SIZE, N_KV_HEADS, HEAD_DIM), 1.0)
    cache_idx_k = nrm(ks[4], (L, n_pool, PAGE_SIZE, IDX_DIM), 1.0)
    state_conv = nrm(ks[5], (L, DEC_BATCH, CONV_W - 1, D_RNN), 1.0)
    state_h = nrm(ks[6], (L, DEC_BATCH, D_RNN), 0.5)
    page_table = jax.random.permutation(ks[7], n_pool)[:DEC_BATCH * n_pages].reshape(DEC_BATCH, n_pages).astype(jnp.int32)
    w_in = nrm(ks[8], (L, D_MODEL, D_IN), D_MODEL ** -0.5)
    conv_w = nrm(ks[9], (L, CONV_W, D_RNN), CONV_W ** -0.5)
    conv_b = nrm(ks[10], (L, D_RNN), 0.01)
    w_a = nrm(ks[11], (L, LRU_BLOCKS, LRU_BLOCK_W, LRU_BLOCK_W), LRU_BLOCK_W ** -0.5)
    b_a = nrm(ks[12], (L, LRU_BLOCKS, LRU_BLOCK_W), 0.01)
    w_i = nrm(ks[13], (L, LRU_BLOCKS, LRU_BLOCK_W, LRU_BLOCK_W), LRU_BLOCK_W ** -0.5)
    b_i = nrm(ks[14], (L, LRU_BLOCKS, LRU_BLOCK_W), 0.01)
    a_pow_c = jax.random.uniform(ks[15], (L, D_RNN), f32, 0.9, 0.999)
    a0 = a_pow_c ** (1.0 / LRU_C)
    lru_lambda = jnp.log(a0) - jnp.log1p(-a0)
    w_branch = nrm(ks[16], (L, D_RNN + N_HEADS * HEAD_DIM, D_MODEL), D_RNN ** -0.5)
    w_out = nrm(ks[17], (L, D_MODEL, D_MODEL), DN_BETA * D_MODEL ** -0.5)
    ln1_g = 1.0 + nrm(ks[18], (L, D_MODEL), 0.02)
    ln1_b = nrm(ks[19], (L, D_MODEL), 0.02)
    w_router = nrm(ks[20], (L, D_MODEL, N_EXPERTS), D_MODEL ** -0.5)
    router_bias = nrm(ks[21], (L, N_EXPERTS), 0.01)
    w_exp_gate = nrm(ks[22], (L, N_EXPERTS, D_MODEL, D_EXPERT), D_MODEL ** -0.5)
    w_exp_up = nrm(ks[23], (L, N_EXPERTS, D_MODEL, D_EXPERT), D_MODEL ** -0.5)
    w_exp_down = nrm(ks[24], (L, N_EXPERTS, D_EXPERT, D_MODEL), DN_BETA * D_EXPERT ** -0.5)
    w_sh_gate = nrm(ks[25], (L, D_MODEL, D_SHARED), D_MODEL ** -0.5)
    w_sh_up = nrm(ks[26], (L, D_MODEL, D_SHARED), D_MODEL ** -0.5)
    w_sh_down = nrm(ks[27], (L, D_SHARED, D_MODEL), DN_BETA * D_SHARED ** -0.5)
    ln2_g = 1.0 + nrm(ks[28], (L, D_MODEL), 0.02)
    ln2_b = nrm(ks[29], (L, D_MODEL), 0.02)
    return {'x_prompt': x_prompt, 'x_sample': x_sample, 'cache_k': cache_k, 'cache_v': cache_v,
            'cache_idx_k': cache_idx_k, 'state_conv': state_conv, 'state_h': state_h, 'page_table': page_table,
            'w_in': w_in, 'conv_w': conv_w, 'conv_b': conv_b, 'w_a': w_a, 'b_a': b_a, 'w_i': w_i, 'b_i': b_i,
            'lru_lambda': lru_lambda, 'w_branch': w_branch, 'w_out': w_out, 'ln1_g': ln1_g, 'ln1_b': ln1_b,
            'w_router': w_router, 'router_bias': router_bias, 'w_exp_gate': w_exp_gate, 'w_exp_up': w_exp_up,
            'w_exp_down': w_exp_down, 'w_sh_gate': w_sh_gate, 'w_sh_up': w_sh_up, 'w_sh_down': w_sh_down,
            'ln2_g': ln2_g, 'ln2_b': ln2_b}


def reference(x_prompt, x_sample, cache_k, cache_v, cache_idx_k, state_conv, state_h, page_table,
              w_in, conv_w, conv_b, w_a, b_a, w_i, b_i, lru_lambda, w_branch, w_out, ln1_g, ln1_b,
              w_router, router_bias, w_exp_gate, w_exp_up, w_exp_down, w_sh_gate, w_sh_up, w_sh_down,
              ln2_g, ln2_b):
    y_prompt, y_sample = x_prompt, x_sample
    layer_states = []
    for l in range(DEPTH):
        y_prompt, y_sample, st = decoder_layer(
            y_prompt, y_sample, cache_k[l], cache_v[l], cache_idx_k[l], state_conv[l], state_h[l], page_table,
            w_in[l], conv_w[l], conv_b[l], w_a[l], b_a[l], w_i[l], b_i[l], lru_lambda[l], w_branch[l], w_out[l],
            ln1_g[l], ln1_b[l], w_router[l], router_bias[l], w_exp_gate[l], w_exp_up[l], w_exp_down[l],
            w_sh_gate[l], w_sh_up[l], w_sh_down[l], ln2_g[l], ln2_b[l])
        layer_states.append(st)
    (k_prompt, v_prompt, idx_k_prompt, conv_prompt, h_prompt,
     k_sample, v_sample, idx_k_sample, conv_sample, h_sample) = [jnp.stack(c) for c in zip(*layer_states)]
    return (y_prompt, y_sample, k_prompt, v_prompt, idx_k_prompt, conv_prompt, h_prompt,
            k_sample, v_sample, idx_k_sample, conv_sample, h_sample)
```

```python
import functools
import math

import jax
import jax.numpy as jnp
from jax import lax
from jax.experimental import pallas as pl
from jax.experimental.pallas import tpu as pltpu

F32 = jnp.float32
BF16 = jnp.bfloat16
I32 = jnp.int32

LRU_BLOCKS = 16
CONV_W = 4
LRU_C = 8.0
N_HEADS = 16
N_KV_HEADS = 4
HEAD_DIM = 64
ROT_DIM = 16
ROPE_THETA = 500000.0
IDX_HEADS = 8
IDX_DIM = 64
TOPK_KEYS = 256
PAGE_SIZE = 128
N_EXPERT_GROUPS = 8
TOPK_GROUPS = 4
EXPERT_TOP_K = 8
ROUTED_SCALE = 2.5
MOE_BLOCK = 128
LN_EPS = 1e-5

LANES = 128
SUBLANES = 8
VMEM_LIMIT = 56 * 1024 * 1024

NEG_INF = float("-inf")


def _cparams(sem):
    return pltpu.CompilerParams(dimension_semantics=sem, vmem_limit_bytes=VMEM_LIMIT)


def _dot(a, b):
    return jnp.dot(a, b, preferred_element_type=F32)


def _dot_nt(a, b):
    return lax.dot_general(a, b, (((1,), (1,)), ((), ())), preferred_element_type=F32)


_PROJ_GROUPS = (
    ("xr", 1024, 1024, False),
    ("gr", 1024, 1024, False),
    ("q", 1024, 1024, True),
    ("k", 256, 256, True),
    ("v", 256, 256, False),
    ("iq", 512, 512, True),
    ("ik", 64, 128, True),
    ("iw", 8, 128, False),
    ("ga", 1024, 1024, False),
    ("gb", 1024, 1024, False),
)


def _pack_w_in(w_in):
    cols = []
    c0 = 0
    for _, w, wp, _ in _PROJ_GROUPS:
        blk = w_in[:, c0:c0 + w]
        if wp != w:
            blk = jnp.pad(blk, ((0, 0), (0, wp - w)))
        cols.append(blk)
        c0 += w
    return jnp.concatenate(cols, axis=1).astype(BF16)


def _rope_tables(pos):
    half = ROT_DIM // 2
    inv_freq = jnp.power(ROPE_THETA, -jnp.arange(half, dtype=F32) * (2.0 / ROT_DIM))
    ang = pos.astype(F32)[:, None] * inv_freq[None, :]
    cos = jnp.cos(ang)
    sin = jnp.sin(ang)
    t = pos.shape[0]
    rest = HEAD_DIM - ROT_DIM
    c = jnp.concatenate([cos, cos, jnp.ones((t, rest), F32)], axis=1)
    sa = jnp.concatenate([-sin, jnp.zeros((t, half + rest), F32)], axis=1)
    sb = jnp.concatenate([jnp.zeros((t, half), F32), sin, jnp.zeros((t, rest), F32)], axis=1)
    rep = LANES // HEAD_DIM
    return jnp.tile(c, (1, rep)), jnp.tile(sa, (1, rep)), jnp.tile(sb, (1, rep))


def _proj_kernel(x_ref, w_ref, c_ref, sa_ref, sb_ref,
                 xr_ref, gr_ref, q_ref, k_ref, v_ref, iq_ref,
                 ik_ref, ikb_ref, iw_ref, ga_ref, gb_ref, *, iw_scale):
    xb = x_ref[...].astype(BF16)
    c = c_ref[...]
    sa = sa_ref[...]
    sb = sb_ref[...]

    def rope(z):
        n = z.shape[1] // LANES
        cc = jnp.concatenate([c] * n, axis=1) if n > 1 else c
        aa = jnp.concatenate([sa] * n, axis=1) if n > 1 else sa
        bb = jnp.concatenate([sb] * n, axis=1) if n > 1 else sb
        half = ROT_DIM // 2
        up = pltpu.roll(z, z.shape[1] - half, axis=1)
        dn = pltpu.roll(z, half, axis=1)
        return z * cc + up * aa + dn * bb

    c0 = 0
    zs = {}
    for name, _, wp, rot in _PROJ_GROUPS:
        z = _dot(xb, w_ref[:, c0:c0 + wp])
        zs[name] = rope(z) if rot else z
        c0 += wp
    xr_ref[...] = zs["xr"]
    gr_ref[...] = zs["gr"]
    q_ref[...] = zs["q"].astype(BF16)
    k_ref[...] = zs["k"]
    v_ref[...] = zs["v"]
    iq_ref[...] = zs["iq"].astype(BF16)
    ik_ref[...] = zs["ik"][:, :IDX_DIM]
    ikb_ref[...] = zs["ik"].astype(BF16)
    iw_ref[...] = zs["iw"][:, :IDX_HEADS] * iw_scale
    ga_ref[...] = zs["ga"]
    gb_ref[...] = zs["gb"]


def _project(x2d, w_packed, tabs, n_tab_blocks, tm):
    m, d = x2d.shape
    n_total = w_packed.shape[1]
    grid = (m // tm,)
    row = lambda i: (i, 0)
    tab = lambda i: (i % n_tab_blocks, 0)
    out_defs = (
        (1024, F32), (1024, F32), (1024, BF16), (256, F32), (256, F32),
        (512, BF16), (IDX_DIM, F32), (LANES, BF16), (IDX_HEADS, F32), (1024, F32), (1024, F32))
    out_shape = tuple(jax.ShapeDtypeStruct((m, w), dt) for w, dt in out_defs)
    out_specs = tuple(pl.BlockSpec((tm, w), row) for w, _ in out_defs)
    return pl.pallas_call(
        functools.partial(_proj_kernel, iw_scale=IDX_HEADS ** -0.5 * IDX_DIM ** -0.5),
        grid=grid,
        in_specs=[pl.BlockSpec((tm, d), row),
                  pl.BlockSpec((d, n_total), lambda i: (0, 0)),
                  pl.BlockSpec((tm, LANES), tab),
                  pl.BlockSpec((tm, LANES), tab),
                  pl.BlockSpec((tm, LANES), tab)],
        out_specs=out_specs,
        out_shape=out_shape,
        compiler_params=_cparams(("parallel",)),
        name="in_proj",
    )(x2d, w_packed, *tabs)


def _block_diag(w, per):
    n, d, _ = w.shape
    g = n // per
    w4 = w.reshape(g, per, d, d)
    eye = jnp.eye(per, dtype=w.dtype)
    out = jnp.einsum("gpde,pq->gpdqe", w4, eye)
    return out.reshape(g, per * d, per * d)


def _gelu_tanh(x):
    return 0.5 * x * (1.0 + jnp.tanh(math.sqrt(2.0 / math.pi) * (x + 0.044715 * (x * x * x))))


def _lru_gates(xc, wa_ref, ba_ref, wi_ref, bi_ref, clam_ref):
    xb = xc.astype(BF16)
    n_tiles = wa_ref.shape[0]
    tw = wa_ref.shape[1]
    ra, ri = [], []
    for g in range(n_tiles):
        xs = xb[:, g * tw:(g + 1) * tw]
        ra.append(_dot(xs, wa_ref[g]))
        ri.append(_dot(xs, wi_ref[g]))
    r = jax.nn.sigmoid(jnp.concatenate(ra, axis=1) + ba_ref[...])
    ig = jax.nn.sigmoid(jnp.concatenate(ri, axis=1) + bi_ref[...])
    log_a = clam_ref[...] * r
    a = jnp.exp(log_a)
    u = jnp.sqrt(-jnp.tanh(log_a) * (a * a + 1.0)) * (ig * xc)
    return a, u


def _scan8(a3, u3):
    t_idx = lax.broadcasted_iota(I32, a3.shape, 1)
    for d in (1, 2, 4):
        keep = t_idx >= d
        a_sh = jnp.where(keep, pltpu.roll(a3, d, axis=1), 1.0)
        u_sh = jnp.where(keep, pltpu.roll(u3, d, axis=1), 0.0)
        u3 = a3 * u_sh + u3
        a3 = a3 * a_sh
    return a3, u3


def _rglru_prompt_kernel(xr_ref, gr_ref, cw_ref, cb_ref, wa_ref, ba_ref, wi_ref, bi_ref, clam_ref,
                         y_ref, hl_ref, xbuf, hc):
    i = pl.program_id(1)
    tt, d = xr_ref.shape

    @pl.when(i == 0)
    def _():
        xbuf[0:SUBLANES, :] = jnp.zeros((SUBLANES, d), F32)
        hc[...] = jnp.zeros_like(hc)

    x = xr_ref[...]
    xbuf[SUBLANES:SUBLANES + tt, :] = x
    xc = cb_ref[...] + cw_ref[CONV_W - 1:CONV_W, :] * x
    for j in range(CONV_W - 1):
        back = CONV_W - 1 - j
        xc = xc + cw_ref[j:j + 1, :] * xbuf[SUBLANES - back:SUBLANES - back + tt, :]
    xbuf[0:SUBLANES, :] = x[tt - SUBLANES:tt, :]

    a, u = _lru_gates(xc, wa_ref, ba_ref, wi_ref, bi_ref, clam_ref)
    g = tt // SUBLANES
    a3, u3 = _scan8(a.reshape(g, SUBLANES, d), u.reshape(g, SUBLANES, d))
    h_in = hc[...]
    hs = []
    for gi in range(g):
        h8 = a3[gi] * h_in + u3[gi]
        hs.append(h8)
        h_in = jnp.broadcast_to(h8[SUBLANES - 1:SUBLANES, :], (SUBLANES, d))
    hc[...] = h_in
    h = jnp.concatenate(hs, axis=0)
    y_ref[...] = (h * _gelu_tanh(gr_ref[...])).astype(y_ref.dtype)

    @pl.when(i == pl.num_programs(1) - 1)
    def _():
        hl_ref[0] = h_in


def _rglru_prompt(xr, gr, b, t, lru_w, tt):
    d = xr.shape[1]
    nt = t // tt
    row = lambda bi, i: (bi * nt + i, 0)
    full2 = lambda bi, i: (0, 0)
    full3 = lambda bi, i: (0, 0, 0)
    cw, cb, wa, ba, wi, bi_, clam = lru_w
    y, hl = pl.pallas_call(
        _rglru_prompt_kernel,
        grid=(b, nt),
        in_specs=[pl.BlockSpec((tt, d), row), pl.BlockSpec((tt, d), row),
                  pl.BlockSpec(cw.shape, full2), pl.BlockSpec(cb.shape, full2),
                  pl.BlockSpec(wa.shape, full3), pl.BlockSpec(ba.shape, full2),
                  pl.BlockSpec(wi.shape, full3), pl.BlockSpec(bi_.shape, full2),
                  pl.BlockSpec(clam.shape, full2)],
        out_specs=(pl.BlockSpec((tt, d), row),
                   pl.BlockSpec((1, SUBLANES, d), lambda bi, i: (bi, 0, 0))),
        out_shape=(jax.ShapeDtypeStruct((b * t, d), BF16),
                   jax.ShapeDtypeStruct((b, SUBLANES, d), F32)),
        scratch_shapes=[pltpu.VMEM((tt + SUBLANES, d), F32), pltpu.VMEM((SUBLANES, d), F32)],
        compiler_params=_cparams(("parallel", "arbitrary")),
        name="rglru_prompt",
    )(xr, gr, cw, cb, wa, ba, wi, bi_, clam)
    return y, hl[:, 0, :]


def _rglru_sample_kernel(xp_ref, gr_ref, h0_ref, cw_ref, cb_ref, wa_ref, ba_ref, wi_ref, bi_ref,
                         clam_ref, y_ref, hs_ref):
    bt, t, d = gr_ref.shape
    xc = cb_ref[...].reshape(1, 1, d)
    for j in range(CONV_W):
        xc = xc + cw_ref[j:j + 1, :].reshape(1, 1, d) * xp_ref[:, j:j + t, :]
    xc2 = xc.reshape(bt * t, d)
    a, u = _lru_gates(xc2, wa_ref, ba_ref, wi_ref, bi_ref, clam_ref)
    a3, u3 = _scan8(a.reshape(bt, t, d), u.reshape(bt, t, d))
    h = a3 * h0_ref[...] + u3
    hs_ref[...] = h
    y_ref[...] = (h * _gelu_tanh(gr_ref[...])).astype(y_ref.dtype)


def _rglru_sample(xp, gr3, h0, lru_w, bt):
    nb, t, d = gr3.shape
    cw, cb, wa, ba, wi, bi_, clam = lru_w
    full2 = lambda i: (0, 0)
    full3 = lambda i: (0, 0, 0)
    blk = lambda i: (i, 0, 0)
    return pl.pallas_call(
        _rglru_sample_kernel,
        grid=(nb // bt,),
        in_specs=[pl.BlockSpec((bt, t + CONV_W - 1, d), blk), pl.BlockSpec((bt, t, d), blk),
                  pl.BlockSpec((bt, 1, d), blk),
                  pl.BlockSpec(cw.shape, full2), pl.BlockSpec(cb.shape, full2),
                  pl.BlockSpec(wa.shape, full3), pl.BlockSpec(ba.shape, full2),
                  pl.BlockSpec(wi.shape, full3), pl.BlockSpec(bi_.shape, full2),
                  pl.BlockSpec(clam.shape, full2)],
        out_specs=(pl.BlockSpec((bt, t, d), blk), pl.BlockSpec((bt, t, d), blk)),
        out_shape=(jax.ShapeDtypeStruct((nb, t, d), BF16), jax.ShapeDtypeStruct((nb, t, d), F32)),
        compiler_params=_cparams(("parallel",)),
        name="rglru_sample",
    )(xp, gr3, h0, cw, cb, wa, ba, wi, bi_, clam)


def _prep_lru(conv_w, conv_b, w_a, b_a, w_i, b_i, lru_lambda):
    d = conv_w.shape[1]
    per = 256 // w_a.shape[1]
    clam = (-LRU_C * jax.nn.softplus(-lru_lambda.astype(F32))).reshape(1, d)
    return (conv_w, conv_b.reshape(1, d), _block_diag(w_a, per).astype(BF16), b_a.reshape(1, d),
            _block_diag(w_i, per).astype(BF16), b_i.reshape(1, d), clam)


INT_MIN = -2 ** 31
INT_MAX = 2 ** 31 - 1
M_INIT = -1e30


def _expand_pairs(x):
    rows, w = x.shape
    lo = lax.broadcasted_iota(I32, (rows, LANES), 1) < HEAD_DIM
    outs = []
    for j in range(w // LANES):
        blk = x[:, j * LANES:(j + 1) * LANES]
        rol = pltpu.roll(blk, HEAD_DIM, axis=1)
        z = jnp.zeros_like(blk)
        outs += [jnp.where(lo, blk, z), jnp.where(lo, z, rol), jnp.where(lo, rol, z), jnp.where(lo, z, blk)]
    return jnp.concatenate(outs, axis=1).astype(BF16)


def _sortable_key(s):
    bits = lax.bitcast_convert_type(s, I32)
    return jnp.where(bits < 0, bits ^ INT_MAX, bits)


def _lane_fold(w):
    part = w[:, 0:LANES]
    for j in range(1, w.shape[1] // LANES):
        part = part + w[:, j * LANES:(j + 1) * LANES]
    return part


def _sparse_attend(q, iq, iw, qpos, kexp, vexp, ikexp, nck, tk, n_keep, idx_bits,
                   key_ref, bias_ref, m_ref, l_ref, acc_ref, j_ref):
    tq = q.shape[0]
    kf = float(n_keep)
    lane_pos = lax.broadcasted_iota(I32, (tq, tk), 1)

    lhs_i = jnp.concatenate([iq[:, m * LANES:(m + 1) * LANES] for m in range(IDX_HEADS // 2)], axis=0)

    def p1(c, carry):
        start = pl.multiple_of(c * tk, tk)
        ike = ikexp[pl.ds(start, tk), :]
        s_even = _dot_nt(lhs_i, ike[:, :LANES])
        s_odd = _dot_nt(lhs_i, ike[:, LANES:])
        sc = jnp.zeros((tq, tk), F32)
        for m in range(IDX_HEADS // 2):
            sc = sc + jnp.maximum(s_even[m * tq:(m + 1) * tq], 0.0) * iw[:, 2 * m:2 * m + 1]
            sc = sc + jnp.maximum(s_odd[m * tq:(m + 1) * tq], 0.0) * iw[:, 2 * m + 1:2 * m + 2]
        sc = jnp.where(lane_pos + start <= qpos, sc, NEG_INF)
        key_ref[c] = _sortable_key(sc)
        return carry

    lax.fori_loop(0, nck, p1, 0)

    def count(weight):
        def body(c, acc):
            return acc + _lane_fold(weight(key_ref[c], c))
        acc = lax.fori_loop(0, nck, body, jnp.zeros((tq, LANES), F32))
        return jnp.sum(acc, axis=1, keepdims=True)

    c0 = count(lambda kk, c: jnp.where(kk >= 0, 1.0, 0.0))
    thr = jnp.where(c0 >= kf, jnp.zeros((tq, 1), I32), jnp.full((tq, 1), INT_MIN, I32))

    def p2(it, thr):
        cand = thr + jnp.left_shift(jnp.int32(1), 30 - it)
        cnt = count(lambda kk, c: jnp.where(kk >= cand, 1.0, 0.0))
        return jnp.where(cnt >= kf, cand, thr)

    thr = lax.fori_loop(0, 31, p2, thr)

    cnt_ge = count(lambda kk, c: jnp.where(kk >= thr, 1.0, 0.0))
    cnt_gt = count(lambda kk, c: jnp.where(kk > thr, 1.0, 0.0))
    need = kf - cnt_gt
    j_ref[...] = jnp.full((tq, 1), INT_MAX, I32)

    @pl.when(jnp.max(cnt_ge) > kf)
    def _():
        def p3(it, jj):
            cand = jj + jnp.left_shift(jnp.int32(1), idx_bits - 1 - it)
            cnt = count(lambda kk, c: jnp.where(
                kk == thr, jnp.where(lane_pos + c * tk < cand, 1.0, 0.0), 0.0))
            return jnp.where(cnt < need, cand, jj)
        jj = lax.fori_loop(0, idx_bits, p3, jnp.zeros((tq, 1), I32))
        j_ref[...] = jnp.where(cnt_ge > kf, jj, INT_MAX)

    jlast = j_ref[...]

    def p3b(c, carry):
        kk = key_ref[c]
        kpos = lane_pos + c * tk
        tie = jnp.where(kk == thr, jnp.where(kpos <= jlast, 0.0, NEG_INF), NEG_INF)
        bias_ref[c] = jnp.where(kpos <= qpos, jnp.where(kk > thr, 0.0, tie), NEG_INF)
        return carry

    lax.fori_loop(0, nck, p3b, 0)

    n_stack = 2 * N_KV_HEADS
    scale = HEAD_DIM ** -0.5
    lhs = []
    for g in range(N_KV_HEADS):
        blk = jnp.concatenate([q[:, (2 * g) * LANES:(2 * g + 1) * LANES],
                               q[:, (2 * g + 1) * LANES:(2 * g + 2) * LANES]], axis=0)
        lhs.append((blk.astype(F32) * scale).astype(BF16))
    m_ref[...] = jnp.full(m_ref.shape, M_INIT, F32)
    l_ref[...] = jnp.zeros(l_ref.shape, F32)
    acc_ref[...] = jnp.zeros(acc_ref.shape, F32)

    def p4(c, carry):
        start = pl.multiple_of(c * tk, tk)
        b1 = bias_ref[c]
        bias2 = jnp.concatenate([b1, b1], axis=0)
        for g in range(N_KV_HEADS):
            for par in range(2):
                col = (2 * g + par) * LANES
                idx = 2 * g + par
                ke = kexp[pl.ds(start, tk), col:col + LANES]
                ve = vexp[pl.ds(start, tk), col:col + LANES]
                s = _dot_nt(lhs[g], ke) + bias2
                m_old = m_ref[idx]
                m_new = jnp.maximum(m_old, jnp.max(s, axis=1, keepdims=True))
                alpha = jnp.exp(m_old - m_new)
                p = jnp.exp(s - m_new)
                l_ref[idx] = alpha * l_ref[idx] + jnp.sum(p, axis=1, keepdims=True)
                acc_ref[idx] = alpha * acc_ref[idx] + _dot(p.astype(BF16), ve)
                m_ref[idx] = m_new
        return carry

    lax.fori_loop(0, nck, p4, 0)

    outs = []
    for g in range(N_KV_HEADS):
        o_lo = acc_ref[2 * g] / l_ref[2 * g]
        o_hi = acc_ref[2 * g + 1] / l_ref[2 * g + 1]
        o = o_lo + o_hi
        outs.append(o[0:tq])
        outs.append(o[tq:2 * tq])
    return outs


def _attn_scratch(tq, tk, nck_max):
    n_stack = 2 * N_KV_HEADS
    return [pltpu.VMEM((nck_max, tq, tk), I32), pltpu.VMEM((nck_max, tq, tk), F32),
            pltpu.VMEM((n_stack, 2 * tq, 1), F32), pltpu.VMEM((n_stack, 2 * tq, 1), F32),
            pltpu.VMEM((n_stack, 2 * tq, LANES), F32), pltpu.VMEM((tq, 1), I32)]


def _attn_prompt_kernel(q_ref, iq_ref, iw_ref, k_ref, v_ref, ik_ref, o_ref,
                        kexp, vexp, ikexp, key_ref, bias_ref, m_ref, l_ref, acc_ref, j_ref,
                        *, tk, n_keep, idx_bits):
    i = pl.program_id(1)
    tq = q_ref.shape[0]
    t = k_ref.shape[0]

    @pl.when(i == 0)
    def _():
        def fill(c, carry):
            rows = pl.ds(pl.multiple_of(c * tk, tk), tk)
            kexp[rows, :] = _expand_pairs(k_ref[rows, :])
            vexp[rows, :] = _expand_pairs(v_ref[rows, :])
            ik = ik_ref[rows, :].astype(F32)
            ikexp[rows, :] = jnp.concatenate([ik, pltpu.roll(ik, HEAD_DIM, axis=1)], axis=1).astype(BF16)
            return carry
        lax.fori_loop(0, t // tk, fill, 0)

    qpos = i * tq + lax.broadcasted_iota(I32, (tq, 1), 0)
    nck = lax.div((i + 1) * tq - 1, tk) + 1
    outs = _sparse_attend(q_ref[...], iq_ref[...], iw_ref[...], qpos, kexp, vexp, ikexp, nck, tk,
                          n_keep, idx_bits, key_ref, bias_ref, m_ref, l_ref, acc_ref, j_ref)
    o_ref[...] = jnp.concatenate(outs, axis=1).astype(o_ref.dtype)


def _attn_prompt(q, iq, iw, k, v, ikb, b, t, tq, tk):
    nq = t // tq
    n_keep = min(TOPK_KEYS, t // 4)
    row = lambda bi, i: (bi * nq + i, 0)
    per_b = lambda bi, i: (bi, 0)
    dq, dk = q.shape[1], k.shape[1]
    kern = functools.partial(_attn_prompt_kernel, tk=tk, n_keep=n_keep,
                             idx_bits=max(1, math.ceil(math.log2(t))))
    return pl.pallas_call(
        kern,
        grid=(b, nq),
        in_specs=[pl.BlockSpec((tq, dq), row), pl.BlockSpec((tq, iq.shape[1]), row),
                  pl.BlockSpec((tq, iw.shape[1]), row),
                  pl.BlockSpec((t, dk), per_b), pl.BlockSpec((t, dk), per_b),
                  pl.BlockSpec((t, LANES), per_b)],
        out_specs=pl.BlockSpec((tq, dq), row),
        out_shape=jax.ShapeDtypeStruct((b * t, dq), BF16),
        scratch_shapes=[pltpu.VMEM((t, 4 * dk), BF16), pltpu.VMEM((t, 4 * dk), BF16),
                        pltpu.VMEM((t, 2 * LANES), BF16)] + _attn_scratch(tq, tk, t // tk),
        compiler_params=_cparams(("parallel", "arbitrary")),
        name="attn_prompt",
    )(q, iq, iw, k, v, ikb)


SAMPLE_Q_ROWS = 16


def _attn_sample_kernel(pt_ref, q_ref, iq_ref, iw_ref, kn_ref, vn_ref, ikn_ref, *rest,
                        n_pages, t_new, tk, n_keep, idx_bits):
    kpages = rest[:n_pages]
    vpages = rest[n_pages:2 * n_pages]
    ikpages = rest[2 * n_pages:3 * n_pages]
    o_ref = rest[3 * n_pages]
    kexp, vexp, ikexp, key_ref, bias_ref, m_ref, l_ref, acc_ref, j_ref = rest[3 * n_pages + 1:]
    tq = q_ref.shape[0]
    page = kpages[0].shape[1]
    past = n_pages * page
    s_pad = kexp.shape[0]
    tail = s_pad - past

    r_i = lax.broadcasted_iota(I32, (IDX_DIM, 2 * LANES), 0)
    c_i = lax.broadcasted_iota(I32, (IDX_DIM, 2 * LANES), 1)
    place = jnp.where((c_i == r_i) | (c_i == r_i + LANES + HEAD_DIM), 1.0, 0.0).astype(BF16)

    for p in range(n_pages):
        rows = slice(p * page, (p + 1) * page)
        kexp[rows, :] = _expand_pairs(kpages[p][0])
        vexp[rows, :] = _expand_pairs(vpages[p][0])
        ikexp[rows, :] = _dot(ikpages[p][0].astype(BF16), place).astype(BF16)

    def padded(x):
        return jnp.concatenate([x, jnp.zeros((tail - t_new, x.shape[1]), x.dtype)], axis=0)

    kexp[past:s_pad, :] = _expand_pairs(padded(kn_ref[...]))
    vexp[past:s_pad, :] = _expand_pairs(padded(vn_ref[...]))
    ikexp[past:s_pad, :] = _dot(padded(ikn_ref[...]).astype(BF16), place).astype(BF16)

    r = lax.broadcasted_iota(I32, (tq, 1), 0)
    qpos = past + jnp.where(r < t_new, r, 0)
    outs = _sparse_attend(q_ref[...], iq_ref[...], iw_ref[...], qpos, kexp, vexp, ikexp, s_pad // tk, tk,
                          n_keep, idx_bits, key_ref, bias_ref, m_ref, l_ref, acc_ref, j_ref)
    o_ref[...] = jnp.concatenate(outs, axis=1).astype(o_ref.dtype)


def _attn_sample(q, iq, iw, k_new, v_new, ik_new, cache_k, cache_v, cache_ik, page_table, t_new, tk):
    nb, n_pages = page_table.shape
    n_pool, page, dk = cache_k.shape
    past = n_pages * page
    s_pad = -(-(past + t_new) // tk) * tk
    n_keep = min(TOPK_KEYS, (past + t_new) // 4)
    tq = SAMPLE_Q_ROWS
    dq = q.shape[1]
    row = lambda b, pt: (b, 0)

    def page_spec(width, p):
        return pl.BlockSpec((1, page, width), lambda b, pt, p=p: (pt[b, p], 0, 0))

    in_specs = [pl.BlockSpec((tq, dq), row), pl.BlockSpec((tq, iq.shape[1]), row),
                pl.BlockSpec((tq, iw.shape[1]), row),
                pl.BlockSpec((t_new, dk), row), pl.BlockSpec((t_new, dk), row),
                pl.BlockSpec((t_new, IDX_DIM), row)]
    in_specs += [page_spec(dk, p) for p in range(n_pages)]
    in_specs += [page_spec(dk, p) for p in range(n_pages)]
    in_specs += [page_spec(IDX_DIM, p) for p in range(n_pages)]
    kern = functools.partial(_attn_sample_kernel, n_pages=n_pages, t_new=t_new, tk=tk, n_keep=n_keep,
                             idx_bits=max(1, math.ceil(math.log2(s_pad))))
    grid_spec = pltpu.PrefetchScalarGridSpec(
        num_scalar_prefetch=1,
        grid=(nb,),
        in_specs=in_specs,
        out_specs=pl.BlockSpec((tq, dq), row),
        scratch_shapes=[pltpu.VMEM((s_pad, 4 * dk), BF16), pltpu.VMEM((s_pad, 4 * dk), BF16),
                        pltpu.VMEM((s_pad, 2 * LANES), BF16)] + _attn_scratch(tq, tk, s_pad // tk))
    return pl.pallas_call(
        kern,
        grid_spec=grid_spec,
        out_shape=jax.ShapeDtypeStruct((nb * tq, dq), BF16),
        compiler_params=_cparams(("parallel",)),
        name="attn_sample",
    )(page_table, q, iq, iw, k_new, v_new, ik_new,
      *([cache_k] * n_pages), *([cache_v] * n_pages), *([cache_ik] * n_pages))


def _layer_norm(x, g, b):
    mu = jnp.mean(x, axis=-1, keepdims=True)
    xc = x - mu
    var = jnp.mean(xc * xc, axis=-1, keepdims=True)
    return xc * lax.rsqrt(var + LN_EPS) * g + b


def _merge_kernel(yr_ref, ya_ref, ga_ref, gb_ref, x_ref, wa_ref, wb_ref, wo_ref, g_ref, b_ref, h_ref,
                  *, alpha):
    pa = _dot(yr_ref[...], wa_ref[...])
    pb = _dot(ya_ref[...], wb_ref[...])
    m = jax.nn.sigmoid(ga_ref[...]) * pa + jax.nn.sigmoid(gb_ref[...]) * pb
    mix = _dot(m.astype(BF16), wo_ref[...])
    h_ref[...] = _layer_norm(alpha * x_ref[...] + mix, g_ref[...], b_ref[...])


def _merge(y_rnn, y_att, ga, gb, x2d, wbr_a, wbr_b, w_out, ln_g, ln_b, alpha, tm):
    n, d = x2d.shape
    row = lambda i: (i, 0)
    full = lambda i: (0, 0)
    return pl.pallas_call(
        functools.partial(_merge_kernel, alpha=alpha),
        grid=(n // tm,),
        in_specs=[pl.BlockSpec((tm, d), row)] * 5 + [pl.BlockSpec((d, d), full)] * 3
        + [pl.BlockSpec((1, d), full)] * 2,
        out_specs=pl.BlockSpec((tm, d), row),
        out_shape=jax.ShapeDtypeStruct((n, d), F32),
        compiler_params=_cparams(("parallel",)),
        name="merge_ln1",
    )(y_rnn, y_att, ga, gb, x2d, wbr_a, wbr_b, w_out, ln_g, ln_b)


def _router_kernel(h_ref, whi_ref, wlo_ref, bias_ref, e_ref, g_ref):
    h = h_ref[...]
    tm = h.shape[0]
    ne = whi_ref.shape[1]
    h_hi = h.astype(BF16)
    h_lo = (h - h_hi.astype(F32)).astype(BF16)
    logits = _dot(h_hi, whi_ref[...]) + (_dot(h_lo, whi_ref[...]) + _dot(h_hi, wlo_ref[...]))
    scores = jax.nn.sigmoid(logits)
    biased = scores + bias_ref[...]
    lane = lax.broadcasted_iota(I32, (tm, ne), 1)
    lane_f = lane.astype(F32)
    per_group = ne // N_EXPERT_GROUPS
    big = float(ne)

    def first_argmax(v):
        m = jnp.max(v, axis=1, keepdims=True)
        idx = jnp.min(jnp.where(v == m, lane_f, big), axis=1, keepdims=True)
        return m, idx

    gscore = []
    for g in range(N_EXPERT_GROUPS):
        in_g = (lane >= g * per_group) & (lane < (g + 1) * per_group)
        mg = jnp.where(in_g, biased, NEG_INF)
        m1, i1 = first_argmax(mg)
        m2 = jnp.max(jnp.where(lane_f == i1, NEG_INF, mg), axis=1, keepdims=True)
        gscore.append(m1 + m2)

    ok_map = jnp.zeros((tm, ne), F32)
    for g in range(N_EXPERT_GROUPS):
        rank = jnp.zeros((tm, 1), F32)
        for o in range(N_EXPERT_GROUPS):
            if o == g:
                continue
            ahead = (gscore[o] > gscore[g]) if o > g else (gscore[o] >= gscore[g])
            rank = rank + jnp.where(ahead, 1.0, 0.0)
        in_g = (lane >= g * per_group) & (lane < (g + 1) * per_group)
        ok_map = jnp.where(in_g, jnp.where(rank < float(TOPK_GROUPS), 1.0, 0.0), ok_map)

    cur = jnp.where(ok_map > 0.5, biased, NEG_INF)
    out_lane = lax.broadcasted_iota(I32, (tm, LANES), 1)
    e_out = jnp.zeros((tm, LANES), F32)
    s_out = jnp.zeros((tm, LANES), F32)
    total = jnp.zeros((tm, 1), F32)
    for j in range(EXPERT_TOP_K):
        _, idx = first_argmax(cur)
        hit = lane_f == idx
        sel = jnp.sum(jnp.where(hit, scores, 0.0), axis=1, keepdims=True)
        cur = jnp.where(hit, NEG_INF, cur)
        e_out = jnp.where(out_lane == j, idx, e_out)
        s_out = jnp.where(out_lane == j, sel, s_out)
        total = total + sel
    e_ref[...] = e_out
    g_ref[...] = ROUTED_SCALE * s_out / total


def _router(h, w_hi, w_lo, bias, tm):
    n, d = h.shape
    ne = w_hi.shape[1]
    row = lambda i: (i, 0)
    full = lambda i: (0, 0)
    return pl.pallas_call(
        _router_kernel,
        grid=(n // tm,),
        in_specs=[pl.BlockSpec((tm, d), row), pl.BlockSpec((d, ne), full), pl.BlockSpec((d, ne), full),
                  pl.BlockSpec((1, ne), full)],
        out_specs=(pl.BlockSpec((tm, LANES), row), pl.BlockSpec((tm, LANES), row)),
        out_shape=(jax.ShapeDtypeStruct((n, LANES), F32), jax.ShapeDtypeStruct((n, LANES), F32)),
        compiler_params=_cparams(("parallel",)),
        name="router",
    )(h, w_hi, w_lo, bias)


def _dispatch(experts, gates, n_experts):
    n, top_k = experts.shape
    n_assign = n * top_k
    n_rows = -(-(n_assign + n_experts * (MOE_BLOCK - 1)) // MOE_BLOCK) * MOE_BLOCK
    n_blocks = n_rows // MOE_BLOCK
    expert_flat = experts.reshape(n_assign)
    token_flat = jnp.arange(n_assign, dtype=I32) // top_k
    gate_flat = gates.reshape(n_assign)
    order = jnp.argsort(expert_flat)
    e_sorted = expert_flat[order]
    counts = jnp.zeros((n_experts,), I32).at[expert_flat].add(1)
    padded = (counts + MOE_BLOCK - 1) // MOE_BLOCK * MOE_BLOCK
    pad_end = jnp.cumsum(padded)
    raw_start = jnp.cumsum(counts) - counts
    dest = (pad_end - padded)[e_sorted] + jnp.arange(n_assign, dtype=I32) - raw_start[e_sorted]
    row_token = jnp.zeros((n_rows,), I32).at[dest].set(token_flat[order])
    row_gate = jnp.zeros((n_rows,), F32).at[dest].set(gate_flat[order])
    dest_by_assign = jnp.zeros((n_assign,), I32).at[order].set(dest)
    block_start = jnp.arange(n_blocks, dtype=I32) * MOE_BLOCK
    block_expert = jnp.minimum(jnp.searchsorted(pad_end, block_start, side="right"), n_experts - 1).astype(I32)
    n_active = (pad_end[-1] // MOE_BLOCK).astype(I32).reshape(1)
    return row_token, row_gate, dest_by_assign, block_expert, n_active, n_blocks


def _experts_kernel(be_ref, na_ref, tok_ref, gate_ref, h_hbm, wg_ref, wu_ref, wd_ref, o_ref, xbuf, sem):
    i = pl.program_id(0)
    rows = xbuf.shape[0]

    @pl.when(i < na_ref[0])
    def _():
        def issue(r, carry):
            tok = tok_ref[0, 0, r]
            pltpu.make_async_copy(h_hbm.at[pl.ds(tok, 1), :], xbuf.at[pl.ds(r, 1), :], sem).start()
            return carry
        lax.fori_loop(0, rows, issue, 0)
        wg = wg_ref[0].astype(BF16)
        wu = wu_ref[0].astype(BF16)
        wd = wd_ref[0].astype(BF16)
        pltpu.make_async_copy(h_hbm.at[pl.ds(0, rows), :], xbuf, sem).wait()
        xb = xbuf[...].astype(BF16)
        g = _dot(xb, wg)
        u = _dot(xb, wu)
        hb = (g * jax.nn.sigmoid(g)) * u
        o_ref[...] = _dot(hb.astype(BF16), wd) * gate_ref[...]

    @pl.when(i >= na_ref[0])
    def _():
        o_ref[...] = jnp.zeros(o_ref.shape, o_ref.dtype)


def _experts(h, row_token, row_gate, block_expert, n_active, n_blocks, w_gate, w_up, w_down):
    n, d = h.shape
    ne, _, de = w_gate.shape
    n_rows = n_blocks * MOE_BLOCK
    grid_spec = pltpu.PrefetchScalarGridSpec(
        num_scalar_prefetch=2,
        grid=(n_blocks,),
        in_specs=[pl.BlockSpec((1, 1, MOE_BLOCK), lambda i, be, na: (i, 0, 0), memory_space=pltpu.SMEM),
                  pl.BlockSpec((MOE_BLOCK, 1), lambda i, be, na: (i, 0)),
                  pl.BlockSpec(memory_space=pl.ANY),
                  pl.BlockSpec((1, d, de), lambda i, be, na: (be[i], 0, 0)),
                  pl.BlockSpec((1, d, de), lambda i, be, na: (be[i], 0, 0)),
                  pl.BlockSpec((1, de, d), lambda i, be, na: (be[i], 0, 0))],
        out_specs=pl.BlockSpec((MOE_BLOCK, d), lambda i, be, na: (i, 0)),
        scratch_shapes=[pltpu.VMEM((MOE_BLOCK, d), F32), pltpu.SemaphoreType.DMA(())])
    return pl.pallas_call(
        _experts_kernel,
        grid_spec=grid_spec,
        out_shape=jax.ShapeDtypeStruct((n_rows, d), F32),
        compiler_params=_cparams(("arbitrary",)),
        name="moe_experts",
    )(block_expert, n_active, row_token.reshape(n_blocks, 1, MOE_BLOCK), row_gate.reshape(n_rows, 1),
      h, w_gate, w_up, w_down)


def _combine_kernel(dest_ref, rows_hbm, h_ref, wsg_ref, wsu_ref, wsd_ref, g_ref, b_ref, y_ref, gbuf, sem,
                    *, alpha, top_k):
    tm = h_ref.shape[0]
    shift = top_k.bit_length() - 1

    def issue(r, carry):
        dst = lax.shift_right_logical(r, shift) + (r & (top_k - 1)) * tm
        pltpu.make_async_copy(rows_hbm.at[pl.ds(dest_ref[0, 0, r], 1), :], gbuf.at[pl.ds(dst, 1), :], sem).start()
        return carry

    lax.fori_loop(0, tm * top_k, issue, 0)
    h = h_ref[...]
    hb = h.astype(BF16)
    sg = _dot(hb, wsg_ref[...])
    su = _dot(hb, wsu_ref[...])
    shared = _dot(((sg * jax.nn.sigmoid(sg)) * su).astype(BF16), wsd_ref[...])
    pltpu.make_async_copy(rows_hbm.at[pl.ds(0, tm * top_k), :], gbuf, sem).wait()
    routed = gbuf[0:tm, :]
    for j in range(1, top_k):
        routed = routed + gbuf[j * tm:(j + 1) * tm, :]
    y_ref[...] = _layer_norm(alpha * h + (routed + shared), g_ref[...], b_ref[...])


def _combine(h, rows, dest_by_assign, wsg, wsu, wsd, ln_g, ln_b, alpha, top_k, tm):
    n, d = h.shape
    ds_ = wsg.shape[1]
    row = lambda i: (i, 0)
    full = lambda i: (0, 0)
    return pl.pallas_call(
        functools.partial(_combine_kernel, alpha=alpha, top_k=top_k),
        grid=(n // tm,),
        in_specs=[pl.BlockSpec((1, 1, tm * top_k), lambda i: (i, 0, 0), memory_space=pltpu.SMEM),
                  pl.BlockSpec(memory_space=pl.ANY),
                  pl.BlockSpec((tm, d), row),
                  pl.BlockSpec((d, ds_), full), pl.BlockSpec((d, ds_), full), pl.BlockSpec((ds_, d), full),
                  pl.BlockSpec((1, d), full), pl.BlockSpec((1, d), full)],
        out_specs=pl.BlockSpec((tm, d), row),
        out_shape=jax.ShapeDtypeStruct((n, d), F32),
        scratch_shapes=[pltpu.VMEM((tm * top_k, d), F32), pltpu.SemaphoreType.DMA(())],
        compiler_params=_cparams(("arbitrary",)),
        name="moe_combine",
    )(dest_by_assign.reshape(n // tm, 1, tm * top_k), rows, h, wsg, wsu, wsd, ln_g, ln_b)


def _layer(xp, xs, cache_k, cache_v, cache_idx_k, state_conv, state_h, page_table, depth,
           w_in, conv_w, conv_b, w_a, b_a, w_i, b_i, lru_lambda, w_branch, w_out, ln1_g, ln1_b,
           w_router, router_bias, w_exp_gate, w_exp_up, w_exp_down, w_sh_gate, w_sh_up, w_sh_down,
           ln2_g, ln2_b):
    bp, tp, d = xp.shape
    bs, ts, _ = xs.shape
    n_pages = page_table.shape[1]
    past = n_pages * PAGE_SIZE
    alpha = (2.0 * depth) ** 0.25
    d_rnn = conv_w.shape[1]
    dkv = N_KV_HEADS * HEAD_DIM

    w_packed = _pack_w_in(w_in)
    lru_w = _prep_lru(conv_w, conv_b, w_a, b_a, w_i, b_i, lru_lambda)

    np_ = bp * tp
    tabs_p = _rope_tables(jnp.arange(tp, dtype=I32))
    tm = 256
    (xr, gr, q, k_p, v_p, iq, ik_p, ikb, iw, ga_p, gb_p) = _project(
        xp.reshape(np_, d), w_packed, tabs_p, tp // tm, tm)
    y_rnn_p, h_p = _rglru_prompt(xr, gr, bp, tp, lru_w, 256)
    conv_p = xr.reshape(bp, tp, d_rnn)[:, tp - (CONV_W - 1):, :]
    y_att_p = _attn_prompt(q, iq, iw, k_p, v_p, ikb, bp, tp, 128, 512)

    ns = bs * ts
    pos_s = past + jnp.arange(ts, dtype=I32)
    tabs_s = tuple(jnp.tile(t_, (ns // ts, 1)) for t_ in _rope_tables(pos_s))
    (xr_s, gr_s, q_s, k_s, v_s, iq_s, ik_s, _, iw_s, ga_s, gb_s) = _project(
        xs.reshape(ns, d), w_packed, tabs_s, ns // tm, tm)
    xp_s = jnp.concatenate([state_conv.astype(F32), xr_s.reshape(bs, ts, d_rnn)], axis=1)
    y_rnn_s, hs_s = _rglru_sample(xp_s, gr_s.reshape(bs, ts, d_rnn), state_h.reshape(bs, 1, d_rnn), lru_w, 16)
    conv_s = xp_s[:, ts:, :]
    h_s = hs_s[:, ts - 1, :]

    def pad_q(a):
        a3 = a.reshape(bs, ts, a.shape[-1])
        return jnp.pad(a3, ((0, 0), (0, SAMPLE_Q_ROWS - ts), (0, 0))).reshape(bs * SAMPLE_Q_ROWS, a.shape[-1])

    n_pool = cache_k.shape[0]
    y_att_s = _attn_sample(pad_q(q_s), pad_q(iq_s), pad_q(iw_s), k_s, v_s, ik_s,
                           cache_k.reshape(n_pool, PAGE_SIZE, dkv), cache_v.reshape(n_pool, PAGE_SIZE, dkv),
                           cache_idx_k, page_table, ts, 256)
    y_att_s = y_att_s.reshape(bs, SAMPLE_Q_ROWS, d)[:, :ts, :].reshape(ns, d)

    cat = lambda a, b_: jnp.concatenate([a, b_], axis=0)
    wbr = w_branch.astype(BF16)
    h1 = _merge(cat(y_rnn_p, y_rnn_s.reshape(ns, d_rnn)), cat(y_att_p, y_att_s), cat(ga_p, ga_s), cat(gb_p, gb_s),
                cat(xp.reshape(np_, d), xs.reshape(ns, d)), wbr[:d_rnn], wbr[d_rnn:], w_out.astype(BF16),
                ln1_g.reshape(1, d), ln1_b.reshape(1, d), alpha, 256)

    w_r = w_router.astype(F32)
    w_r_hi = w_r.astype(BF16)
    w_r_lo = (w_r - w_r_hi.astype(F32)).astype(BF16)
    ne = w_router.shape[1]
    e_f, g_f = _router(h1, w_r_hi, w_r_lo, router_bias.reshape(1, ne).astype(F32), 128)
    experts = e_f[:, :EXPERT_TOP_K].astype(I32)
    gates = g_f[:, :EXPERT_TOP_K]
    row_token, row_gate, dest, block_expert, n_active, n_blocks = _dispatch(experts, gates, ne)
    rows = _experts(h1, row_token, row_gate, block_expert, n_active, n_blocks, w_exp_gate, w_exp_up, w_exp_down)
    y = _combine(h1, rows, dest, w_sh_gate.astype(BF16), w_sh_up.astype(BF16), w_sh_down.astype(BF16),
                 ln2_g.reshape(1, d), ln2_b.reshape(1, d), alpha, EXPERT_TOP_K, 128)

    yp = y[:np_].reshape(bp, tp, d)
    ys = y[np_:].reshape(bs, ts, d)
    st = (k_p.reshape(bp, tp, N_KV_HEADS, HEAD_DIM), v_p.reshape(bp, tp, N_KV_HEADS, HEAD_DIM),
          ik_p.reshape(bp, tp, IDX_DIM), conv_p, h_p,
          k_s.reshape(bs, ts, N_KV_HEADS, HEAD_DIM), v_s.reshape(bs, ts, N_KV_HEADS, HEAD_DIM),
          ik_s.reshape(bs, ts, IDX_DIM), conv_s, h_s)
    return yp, ys, st


def kernel(x_prompt, x_sample, cache_k, cache_v, cache_idx_k, state_conv, state_h, page_table, w_in, conv_w,
           conv_b, w_a, b_a, w_i, b_i, lru_lambda, w_branch, w_out, ln1_g, ln1_b, w_router, router_bias,
           w_exp_gate, w_exp_up, w_exp_down, w_sh_gate, w_sh_up, w_sh_down, ln2_g, ln2_b):
    depth = w_in.shape[0]
    yp, ys = x_prompt, x_sample
    states = []
    for l in range(depth):
        yp, ys, st = _layer(
            yp, ys, cache_k[l], cache_v[l], cache_idx_k[l], state_conv[l], state_h[l], page_table, depth,
            w_in[l], conv_w[l], conv_b[l], w_a[l], b_a[l], w_i[l], b_i[l], lru_lambda[l], w_branch[l], w_out[l],
            ln1_g[l], ln1_b[l], w_router[l], router_bias[l], w_exp_gate[l], w_exp_up[l], w_exp_down[l],
            w_sh_gate[l], w_sh_up[l], w_sh_down[l], ln2_g[l], ln2_b[l])
        states.append(st)
    stacked = [jnp.stack(c) for c in zip(*states)]
    return (yp, ys, *stacked)
```

```python
import functools
import math

import jax
import jax.numpy as jnp
from jax import lax
from jax.experimental import pallas as pl
from jax.experimental.pallas import tpu as pltpu

F32 = jnp.float32
BF16 = jnp.bfloat16
I32 = jnp.int32

LRU_BLOCKS = 16
CONV_W = 4
LRU_C = 8.0
N_HEADS = 16
N_KV_HEADS = 4
HEAD_DIM = 64
ROT_DIM = 16
ROPE_THETA = 500000.0
IDX_HEADS = 8
IDX_DIM = 64
TOPK_KEYS = 256
PAGE_SIZE = 128
N_EXPERT_GROUPS = 8
TOPK_GROUPS = 4
EXPERT_TOP_K = 8
ROUTED_SCALE = 2.5
MOE_BLOCK = 128
LN_EPS = 1e-5

LANES = 128
SUBLANES = 8
VMEM_LIMIT = 56 * 1024 * 1024

NEG_INF = float("-inf")


def _cparams(sem):
    return pltpu.CompilerParams(dimension_semantics=sem, vmem_limit_bytes=VMEM_LIMIT)


def _dot(a, b):
    return jnp.dot(a, b, preferred_element_type=F32)


def _dot_nt(a, b):
    return lax.dot_general(a, b, (((1,), (1,)), ((), ())), preferred_element_type=F32)


_PROJ_GROUPS = (
    ("xr", 1024, 1024, False),
    ("gr", 1024, 1024, False),
    ("q", 1024, 1024, True),
    ("k", 256, 256, True),
    ("v", 256, 256, False),
    ("iq", 512, 512, True),
    ("ik", 64, 128, True),
    ("iw", 8, 128, False),
    ("ga", 1024, 1024, False),
    ("gb", 1024, 1024, False),
)


def _pack_w_in(w_in):
    cols = []
    c0 = 0
    for _, w, wp, _ in _PROJ_GROUPS:
        blk = w_in[:, c0:c0 + w]
        if wp != w:
            blk = jnp.pad(blk, ((0, 0), (0, wp - w)))
        cols.append(blk)
        c0 += w
    return jnp.concatenate(cols, axis=1).astype(BF16)


def _rope_tables(pos):
    half = ROT_DIM // 2
    inv_freq = jnp.power(ROPE_THETA, -jnp.arange(half, dtype=F32) * (2.0 / ROT_DIM))
    ang = pos.astype(F32)[:, None] * inv_freq[None, :]
    cos = jnp.cos(ang)
    sin = jnp.sin(ang)
    t = pos.shape[0]
    rest = HEAD_DIM - ROT_DIM
    c = jnp.concatenate([cos, cos, jnp.ones((t, rest), F32)], axis=1)
    sa = jnp.concatenate([-sin, jnp.zeros((t, half + rest), F32)], axis=1)
    sb = jnp.concatenate([jnp.zeros((t, half), F32), sin, jnp.zeros((t, rest), F32)], axis=1)
    rep = LANES // HEAD_DIM
    return jnp.tile(c, (1, rep)), jnp.tile(sa, (1, rep)), jnp.tile(sb, (1, rep))


def _proj_kernel(x_ref, w_ref, c_ref, sa_ref, sb_ref,
                 xr_ref, gr_ref, q_ref, k_ref, v_ref, iq_ref,
                 ik_ref, ikb_ref, iw_ref, ga_ref, gb_ref, *, iw_scale):
    xb = x_ref[...].astype(BF16)
    c = c_ref[...]
    sa = sa_ref[...]
    sb = sb_ref[...]

    def rope(z):
        n = z.shape[1] // LANES
        cc = jnp.concatenate([c] * n, axis=1) if n > 1 else c
        aa = jnp.concatenate([sa] * n, axis=1) if n > 1 else sa
        bb = jnp.concatenate([sb] * n, axis=1) if n > 1 else sb
        half = ROT_DIM // 2
        up = pltpu.roll(z, z.shape[1] - half, axis=1)
        dn = pltpu.roll(z, half, axis=1)
        return z * cc + up * aa + dn * bb

    c0 = 0
    zs = {}
    for name, _, wp, rot in _PROJ_GROUPS:
        z = _dot(xb, w_ref[:, c0:c0 + wp])
        zs[name] = rope(z) if rot else z
        c0 += wp
    xr_ref[...] = zs["xr"]
    gr_ref[...] = zs["gr"]
    q_ref[...] = zs["q"].astype(BF16)
    k_ref[...] = zs["k"]
    v_ref[...] = zs["v"]
    iq_ref[...] = zs["iq"].astype(BF16)
    ik_ref[...] = zs["ik"][:, :IDX_DIM]
    ikb_ref[...] = zs["ik"].astype(BF16)
    iw_ref[...] = zs["iw"][:, :IDX_HEADS] * iw_scale
    ga_ref[...] = zs["ga"]
    gb_ref[...] = zs["gb"]


def _project(x2d, w_packed, tabs, n_tab_blocks, tm):
    m, d = x2d.shape
    n_total = w_packed.shape[1]
    grid = (m // tm,)
    row = lambda i: (i, 0)
    tab = lambda i: (i % n_tab_blocks, 0)
    out_defs = (
        (1024, F32), (1024, F32), (1024, BF16), (256, F32), (256, F32),
        (512, BF16), (IDX_DIM, F32), (LANES, BF16), (IDX_HEADS, F32), (1024, F32), (1024, F32))
    out_shape = tuple(jax.ShapeDtypeStruct((m, w), dt) for w, dt in out_defs)
    out_specs = tuple(pl.BlockSpec((tm, w), row) for w, _ in out_defs)
    return pl.pallas_call(
        functools.partial(_proj_kernel, iw_scale=IDX_HEADS ** -0.5 * IDX_DIM ** -0.5),
        grid=grid,
        in_specs=[pl.BlockSpec((tm, d), row),
                  pl.BlockSpec((d, n_total), lambda i: (0, 0)),
                  pl.BlockSpec((tm, LANES), tab),
                  pl.BlockSpec((tm, LANES), tab),
                  pl.BlockSpec((tm, LANES), tab)],
        out_specs=out_specs,
        out_shape=out_shape,
        compiler_params=_cparams(("parallel",)),
        name="in_proj",
    )(x2d, w_packed, *tabs)


def _block_diag(w, per):
    n, d, _ = w.shape
    g = n // per
    w4 = w.reshape(g, per, d, d)
    eye = jnp.eye(per, dtype=w.dtype)
    out = jnp.einsum("gpde,pq->gpdqe", w4, eye)
    return out.reshape(g, per * d, per * d)


def _gelu_tanh(x):
    return 0.5 * x * (1.0 + jnp.tanh(math.sqrt(2.0 / math.pi) * (x + 0.044715 * (x * x * x))))


def _lru_gates(xc, wa_ref, ba_ref, wi_ref, bi_ref, clam_ref):
    xb = xc.astype(BF16)
    n_tiles = wa_ref.shape[0]
    tw = wa_ref.shape[1]
    ra, ri = [], []
    for g in range(n_tiles):
        xs = xb[:, g * tw:(g + 1) * tw]
        ra.append(_dot(xs, wa_ref[g]))
        ri.append(_dot(xs, wi_ref[g]))
    r = jax.nn.sigmoid(jnp.concatenate(ra, axis=1) + ba_ref[...])
    ig = jax.nn.sigmoid(jnp.concatenate(ri, axis=1) + bi_ref[...])
    log_a = clam_ref[...] * r
    a = jnp.exp(log_a)
    u = jnp.sqrt(-jnp.tanh(log_a) * (a * a + 1.0)) * (ig * xc)
    return a, u


def _scan8(a3, u3):
    t_idx = lax.broadcasted_iota(I32, a3.shape, 1)
    for d in (1, 2, 4):
        keep = t_idx >= d
        a_sh = jnp.where(keep, pltpu.roll(a3, d, axis=1), 1.0)
        u_sh = jnp.where(keep, pltpu.roll(u3, d, axis=1), 0.0)
        u3 = a3 * u_sh + u3
        a3 = a3 * a_sh
    return a3, u3


def _rglru_prompt_kernel(xr_ref, gr_ref, cw_ref, cb_ref, wa_ref, ba_ref, wi_ref, bi_ref, clam_ref,
                         y_ref, hl_ref, xbuf, hc):
    i = pl.program_id(1)
    tt, d = xr_ref.shape

    @pl.when(i == 0)
    def _():
        xbuf[0:SUBLANES, :] = jnp.zeros((SUBLANES, d), F32)
        hc[...] = jnp.zeros_like(hc)

    x = xr_ref[...]
    xbuf[SUBLANES:SUBLANES + tt, :] = x
    xc = cb_ref[...] + cw_ref[CONV_W - 1:CONV_W, :] * x
    for j in range(CONV_W - 1):
        back = CONV_W - 1 - j
        xc = xc + cw_ref[j:j + 1, :] * xbuf[SUBLANES - back:SUBLANES - back + tt, :]
    xbuf[0:SUBLANES, :] = x[tt - SUBLANES:tt, :]

    a, u = _lru_gates(xc, wa_ref, ba_ref, wi_ref, bi_ref, clam_ref)
    g = tt // SUBLANES
    a3, u3 = _scan8(a.reshape(g, SUBLANES, d), u.reshape(g, SUBLANES, d))
    h_in = hc[...]
    hs = []
    for gi in range(g):
        h8 = a3[gi] * h_in + u3[gi]
        hs.append(h8)
        h_in = jnp.broadcast_to(h8[SUBLANES - 1:SUBLANES, :], (SUBLANES, d))
    hc[...] = h_in
    h = jnp.concatenate(hs, axis=0)
    y_ref[...] = (h * _gelu_tanh(gr_ref[...])).astype(y_ref.dtype)

    @pl.when(i == pl.num_programs(1) - 1)
    def _():
        hl_ref[0] = h_in


def _rglru_prompt(xr, gr, b, t, lru_w, tt):
    d = xr.shape[1]
    nt = t // tt
    row = lambda bi, i: (bi * nt + i, 0)
    full2 = lambda bi, i: (0, 0)
    full3 = lambda bi, i: (0, 0, 0)
    cw, cb, wa, ba, wi, bi_, clam = lru_w
    y, hl = pl.pallas_call(
        _rglru_prompt_kernel,
        grid=(b, nt),
        in_specs=[pl.BlockSpec((tt, d), row), pl.BlockSpec((tt, d), row),
                  pl.BlockSpec(cw.shape, full2), pl.BlockSpec(cb.shape, full2),
                  pl.BlockSpec(wa.shape, full3), pl.BlockSpec(ba.shape, full2),
                  pl.BlockSpec(wi.shape, full3), pl.BlockSpec(bi_.shape, full2),
                  pl.BlockSpec(clam.shape, full2)],
        out_specs=(pl.BlockSpec((tt, d), row),
                   pl.BlockSpec((1, SUBLANES, d), lambda bi, i: (bi, 0, 0))),
        out_shape=(jax.ShapeDtypeStruct((b * t, d), BF16),
                   jax.ShapeDtypeStruct((b, SUBLANES, d), F32)),
        scratch_shapes=[pltpu.VMEM((tt + SUBLANES, d), F32), pltpu.VMEM((SUBLANES, d), F32)],
        compiler_params=_cparams(("parallel", "arbitrary")),
        name="rglru_prompt",
    )(xr, gr, cw, cb, wa, ba, wi, bi_, clam)
    return y, hl[:, 0, :]


def _rglru_sample_kernel(xp_ref, gr_ref, h0_ref, cw_ref, cb_ref, wa_ref, ba_ref, wi_ref, bi_ref,
                         clam_ref, y_ref, hs_ref):
    bt, t, d = gr_ref.shape
    xc = cb_ref[...].reshape(1, 1, d)
    for j in range(CONV_W):
        xc = xc + cw_ref[j:j + 1, :].reshape(1, 1, d) * xp_ref[:, j:j + t, :]
    xc2 = xc.reshape(bt * t, d)
    a, u = _lru_gates(xc2, wa_ref, ba_ref, wi_ref, bi_ref, clam_ref)
    a3, u3 = _scan8(a.reshape(bt, t, d), u.reshape(bt, t, d))
    h = a3 * h0_ref[...] + u3
    hs_ref[...] = h
    y_ref[...] = (h * _gelu_tanh(gr_ref[...])).astype(y_ref.dtype)


def _rglru_sample(xp, gr3, h0, lru_w, bt):
    nb, t, d = gr3.shape
    cw, cb, wa, ba, wi, bi_, clam = lru_w
    full2 = lambda i: (0, 0)
    full3 = lambda i: (0, 0, 0)
    blk = lambda i: (i, 0, 0)
    return pl.pallas_call(
        _rglru_sample_kernel,
        grid=(nb // bt,),
        in_specs=[pl.BlockSpec((bt, t + CONV_W - 1, d), blk), pl.BlockSpec((bt, t, d), blk),
                  pl.BlockSpec((bt, 1, d), blk),
                  pl.BlockSpec(cw.shape, full2), pl.BlockSpec(cb.shape, full2),
                  pl.BlockSpec(wa.shape, full3), pl.BlockSpec(ba.shape, full2),
                  pl.BlockSpec(wi.shape, full3), pl.BlockSpec(bi_.shape, full2),
                  pl.BlockSpec(clam.shape, full2)],
        out_specs=(pl.BlockSpec((bt, t, d), blk), pl.BlockSpec((bt, t, d), blk)),
        out_shape=(jax.ShapeDtypeStruct((nb, t, d), BF16), jax.ShapeDtypeStruct((nb, t, d), F32)),
        compiler_params=_cparams(("parallel",)),
        name="rglru_sample",
    )(xp, gr3, h0, cw, cb, wa, ba, wi, bi_, clam)


def _prep_lru(conv_w, conv_b, w_a, b_a, w_i, b_i, lru_lambda):
    d = conv_w.shape[1]
    per = 256 // w_a.shape[1]
    clam = (-LRU_C * jax.nn.softplus(-lru_lambda.astype(F32))).reshape(1, d)
    return (conv_w, conv_b.reshape(1, d), _block_diag(w_a, per).astype(BF16), b_a.reshape(1, d),
            _block_diag(w_i, per).astype(BF16), b_i.reshape(1, d), clam)


INT_MIN = -2 ** 31
INT_MAX = 2 ** 31 - 1
M_INIT = -1e30
L_FLOOR = 1e-30


def _expand_pairs(x, ones_lane=False):
    rows, w = x.shape
    lane = lax.broadcasted_iota(I32, (rows, LANES), 1)
    lo = lane < HEAD_DIM
    z_lo = jnp.where(lane == 0, 1.0, 0.0) if ones_lane else jnp.zeros((rows, LANES), F32)
    z_hi = jnp.where(lane == HEAD_DIM, 1.0, 0.0) if ones_lane else jnp.zeros((rows, LANES), F32)
    outs = []
    for j in range(w // LANES):
        blk = x[:, j * LANES:(j + 1) * LANES]
        rol = pltpu.roll(blk, HEAD_DIM, axis=1)
        outs += [jnp.where(lo, blk, z_hi), jnp.where(lo, z_lo, rol),
                 jnp.where(lo, rol, z_hi), jnp.where(lo, z_lo, blk)]
    return jnp.concatenate(outs, axis=1).astype(BF16)


def _sortable_key(s):
    bits = lax.bitcast_convert_type(s, I32)
    return jnp.where(bits < 0, bits ^ INT_MAX, bits)


def _lane_fold(w):
    part = w[:, 0:LANES]
    for j in range(1, w.shape[1] // LANES):
        part = part + w[:, j * LANES:(j + 1) * LANES]
    return part


def _sparse_attend(q, iq, iw, qpos, kexp, vexp, ikexp, nck, tk, n_keep, idx_bits, kmax2, o_ref,
                   key_ref, bias_ref, m_ref, l_ref, acc_ref, j_ref):
    tq = q.shape[0]
    kf = float(n_keep)
    lane_pos = lax.broadcasted_iota(I32, (tq, tk), 1)

    lhs_i = jnp.concatenate([iq[:, m * LANES:(m + 1) * LANES] for m in range(IDX_HEADS // 2)], axis=0)

    def p1(c, carry):
        start = pl.multiple_of(c * tk, tk)
        ike = ikexp[pl.ds(start, tk), :]
        s_even = _dot_nt(lhs_i, ike[:, :LANES])
        s_odd = _dot_nt(lhs_i, ike[:, LANES:])
        sc = jnp.zeros((tq, tk), F32)
        for m in range(IDX_HEADS // 2):
            sc = sc + jnp.maximum(s_even[m * tq:(m + 1) * tq], 0.0) * iw[:, 2 * m:2 * m + 1]
            sc = sc + jnp.maximum(s_odd[m * tq:(m + 1) * tq], 0.0) * iw[:, 2 * m + 1:2 * m + 2]
        sc = jnp.where(lane_pos + start <= qpos, sc, NEG_INF)
        key_ref[c] = _sortable_key(sc)
        return carry

    lax.fori_loop(0, nck, p1, 0)

    def count(weight):
        def body(c, acc):
            return acc + _lane_fold(weight(key_ref[c], c))
        acc = lax.fori_loop(0, nck, body, jnp.zeros((tq, LANES), F32))
        return jnp.sum(acc, axis=1, keepdims=True)

    c0 = count(lambda kk, c: jnp.where(kk >= 0, 1.0, 0.0))
    thr = jnp.where(c0 >= kf, jnp.zeros((tq, 1), I32), jnp.full((tq, 1), INT_MIN, I32))

    def p2(it, thr):
        cand = thr + jnp.left_shift(jnp.int32(1), 30 - it)
        cnt = count(lambda kk, c: jnp.where(kk >= cand, 1.0, 0.0))
        return jnp.where(cnt >= kf, cand, thr)

    thr = lax.fori_loop(0, 31, p2, thr)

    cnt_ge = count(lambda kk, c: jnp.where(kk >= thr, 1.0, 0.0))
    cnt_gt = count(lambda kk, c: jnp.where(kk > thr, 1.0, 0.0))
    need = kf - cnt_gt
    j_ref[...] = jnp.full((tq, 1), INT_MAX, I32)

    @pl.when(jnp.max(cnt_ge) > kf)
    def _():
        def p3(it, jj):
            cand = jj + jnp.left_shift(jnp.int32(1), idx_bits - 1 - it)
            cnt = count(lambda kk, c: jnp.where(
                kk == thr, jnp.where(lane_pos + c * tk < cand, 1.0, 0.0), 0.0))
            return jnp.where(cnt < need, cand, jj)
        jj = lax.fori_loop(0, idx_bits, p3, jnp.zeros((tq, 1), I32))
        j_ref[...] = jnp.where(cnt_ge > kf, jj, INT_MAX)

    jlast = j_ref[...]

    qscale = HEAD_DIM ** -0.5 * math.log2(math.e)
    lhs = []
    for g in range(N_KV_HEADS):
        blk = jnp.concatenate([q[:, (2 * g) * LANES:(2 * g + 1) * LANES],
                               q[:, (2 * g + 1) * LANES:(2 * g + 2) * LANES]], axis=0)
        lhs.append((blk.astype(F32) * qscale).astype(BF16))
    lo_half = lax.broadcasted_iota(I32, (2 * tq, LANES), 1) < HEAD_DIM

    qmax2 = jnp.zeros((tq, 1), F32)
    for g in range(N_KV_HEADS):
        qf = lhs[g].astype(F32)
        sq = qf * qf
        n2 = jnp.maximum(jnp.sum(jnp.where(lo_half, sq, 0.0), axis=1, keepdims=True),
                         jnp.sum(jnp.where(lo_half, 0.0, sq), axis=1, keepdims=True))
        qmax2 = jnp.maximum(qmax2, jnp.maximum(n2[0:tq], n2[tq:2 * tq]))
    neg_shift = -jnp.sqrt(qmax2 * kmax2)

    def p3b(c, carry):
        kk = key_ref[c]
        kpos = lane_pos + c * tk
        tie = jnp.where(kk == thr, jnp.where(kpos <= jlast, neg_shift, NEG_INF), NEG_INF)
        bias_ref[c] = jnp.where(kpos <= qpos, jnp.where(kk > thr, neg_shift, tie), NEG_INF)
        return carry

    lax.fori_loop(0, nck, p3b, 0)

    def stacks(c):
        start = pl.multiple_of(c * tk, tk)
        b1 = bias_ref[c]
        bias2 = jnp.concatenate([b1, b1], axis=0)
        for g in range(N_KV_HEADS):
            for par in range(2):
                col = (2 * g + par) * LANES
                ke = kexp[pl.ds(start, tk), col:col + LANES]
                ve = vexp[pl.ds(start, tk), col:col + LANES]
                yield 2 * g + par, _dot_nt(lhs[g], ke) + bias2, ve

    def assemble(denoms):
        outs = []
        for g in range(N_KV_HEADS):
            l_lo, l_hi = denoms(g)
            o = (jnp.where(lo_half, acc_ref[2 * g], 0.0) / l_lo
                 + jnp.where(lo_half, 0.0, acc_ref[2 * g + 1]) / l_hi)
            outs.append(o[0:tq])
            outs.append(o[tq:2 * tq])
        o_ref[...] = jnp.concatenate(outs, axis=1).astype(o_ref.dtype)

    def attend_exact():
        m_ref[...] = jnp.full(m_ref.shape, M_INIT, F32)
        l_ref[...] = jnp.zeros(l_ref.shape, F32)
        acc_ref[...] = jnp.zeros(acc_ref.shape, F32)

        def body(c, carry):
            for idx, s, ve in stacks(c):
                m_old = m_ref[idx]
                m_new = jnp.maximum(m_old, jnp.max(s, axis=1, keepdims=True))
                alpha = jnp.exp2(m_old - m_new)
                p = jnp.exp2(s - m_new)
                l_ref[idx] = alpha * l_ref[idx] + jnp.sum(p, axis=1, keepdims=True)
                acc_ref[idx] = alpha * acc_ref[idx] + _dot(p.astype(BF16), ve)
                m_ref[idx] = m_new
            return carry

        lax.fori_loop(0, nck, body, 0)
        assemble(lambda g: (l_ref[2 * g], l_ref[2 * g + 1]))

    acc_ref[...] = jnp.zeros(acc_ref.shape, F32)

    def body(c, carry):
        for idx, s, ve in stacks(c):
            acc_ref[idx] = acc_ref[idx] + _dot(jnp.exp2(s).astype(BF16), ve)
        return carry

    lax.fori_loop(0, nck, body, 0)

    def denoms(g):
        return acc_ref[2 * g][:, HEAD_DIM:HEAD_DIM + 1], acc_ref[2 * g + 1][:, 0:1]

    lmin = jnp.full((2 * tq, 1), jnp.inf, F32)
    for g in range(N_KV_HEADS):
        l_lo, l_hi = denoms(g)
        lmin = jnp.minimum(lmin, jnp.minimum(l_lo, l_hi))
    healthy = jnp.min(lmin) >= L_FLOOR
    assemble(denoms)

    @pl.when(jnp.logical_not(healthy))
    def _():
        attend_exact()


def _attn_scratch(tq, tk, nck_max):
    n_stack = 2 * N_KV_HEADS
    return [pltpu.VMEM((nck_max, tq, tk), I32), pltpu.VMEM((nck_max, tq, tk), F32),
            pltpu.VMEM((n_stack, 2 * tq, 1), F32), pltpu.VMEM((n_stack, 2 * tq, 1), F32),
            pltpu.VMEM((n_stack, 2 * tq, LANES), F32), pltpu.VMEM((tq, 1), I32)]


def _attn_prompt_kernel(q_ref, iq_ref, iw_ref, k_ref, v_ref, ik_ref, o_ref,
                        kexp, vexp, ikexp, kmax_ref, key_ref, bias_ref, m_ref, l_ref, acc_ref, j_ref,
                        *, tk, n_keep, idx_bits):
    i = pl.program_id(1)
    tq = q_ref.shape[0]
    t = k_ref.shape[0]

    @pl.when(i == 0)
    def _():
        kmax_ref[0] = jnp.float32(0.0)

        def fill(c, carry):
            rows = pl.ds(pl.multiple_of(c * tk, tk), tk)
            ke = _expand_pairs(k_ref[rows, :])
            kexp[rows, :] = ke
            n2 = jnp.zeros((tk, 1), F32)
            for g in range(N_KV_HEADS):
                kf = ke[:, 2 * g * LANES:(2 * g + 1) * LANES].astype(F32)
                n2 = jnp.maximum(n2, jnp.sum(kf * kf, axis=1, keepdims=True))
            kmax_ref[0] = jnp.maximum(kmax_ref[0], jnp.max(n2))
            vexp[rows, :] = _expand_pairs(v_ref[rows, :], ones_lane=True)
            ik = ik_ref[rows, :].astype(F32)
            ikexp[rows, :] = jnp.concatenate([ik, pltpu.roll(ik, HEAD_DIM, axis=1)], axis=1).astype(BF16)
            return carry
        lax.fori_loop(0, t // tk, fill, 0)

    qpos = i * tq + lax.broadcasted_iota(I32, (tq, 1), 0)
    nck = lax.div((i + 1) * tq - 1, tk) + 1
    _sparse_attend(q_ref[...], iq_ref[...], iw_ref[...], qpos, kexp, vexp, ikexp, nck, tk,
                   n_keep, idx_bits, kmax_ref[0], o_ref, key_ref, bias_ref, m_ref, l_ref, acc_ref, j_ref)


def _attn_prompt(q, iq, iw, k, v, ikb, b, t, tq, tk):
    nq = t // tq
    n_keep = min(TOPK_KEYS, t // 4)
    row = lambda bi, i: (bi * nq + i, 0)
    per_b = lambda bi, i: (bi, 0)
    dq, dk = q.shape[1], k.shape[1]
    kern = functools.partial(_attn_prompt_kernel, tk=tk, n_keep=n_keep,
                             idx_bits=max(1, math.ceil(math.log2(t))))
    return pl.pallas_call(
        kern,
        grid=(b, nq),
        in_specs=[pl.BlockSpec((tq, dq), row), pl.BlockSpec((tq, iq.shape[1]), row),
                  pl.BlockSpec((tq, iw.shape[1]), row),
                  pl.BlockSpec((t, dk), per_b), pl.BlockSpec((t, dk), per_b),
                  pl.BlockSpec((t, LANES), per_b)],
        out_specs=pl.BlockSpec((tq, dq), row),
        out_shape=jax.ShapeDtypeStruct((b * t, dq), BF16),
        scratch_shapes=[pltpu.VMEM((t, 4 * dk), BF16), pltpu.VMEM((t, 4 * dk), BF16),
                        pltpu.VMEM((t, 2 * LANES), BF16), pltpu.SMEM((1,), F32)]
        + _attn_scratch(tq, tk, t // tk),
        compiler_params=_cparams(("parallel", "arbitrary")),
        name="attn_prompt",
    )(q, iq, iw, k, v, ikb)


SAMPLE_Q_ROWS = 16


def _sample_geometry(n_pages, page, t_new):
    assert N_HEADS * t_new == LANES and IDX_HEADS * t_new == IDX_DIM and t_new <= page
    past = n_pages * page
    return past, past + page, LANES // t_new


def _with_zero_tail(x, rows):
    return jnp.concatenate([x, jnp.zeros((rows - x.shape[0], x.shape[1]), x.dtype)], axis=0)


def _sample_scores_kernel(pt_ref, iq_ref, w_ref, ikn_ref, *rest, n_pages, t_new, group):
    ikpages = rest[:n_pages]
    o_ref, ikall = rest[n_pages:]
    bl = lax.rem(pl.program_id(0), group)
    page = ikpages[0].shape[1]
    past = n_pages * page
    for p in range(n_pages):
        ikall[p * page:(p + 1) * page, :] = ikpages[p][0].astype(BF16)
    ikall[past:past + page, :] = _with_zero_tail(ikn_ref[...], page).astype(BF16)

    iqf = iq_ref[...].astype(F32)[0:t_new]
    pieces = []
    for h in range(IDX_HEADS):
        blk = iqf[:, (h // 2) * LANES:(h // 2 + 1) * LANES]
        if h % 2:
            blk = pltpu.roll(blk, IDX_DIM, axis=1)
        pieces.append(blk[:, :IDX_DIM])
    iqm = jnp.concatenate(pieces, axis=0).astype(BF16)
    r = jnp.maximum(_dot_nt(ikall[...], iqm), 0.0) * w_ref[0]
    hi = r.astype(BF16)
    lo = (r - hi.astype(F32)).astype(BF16)
    r_i = lax.broadcasted_iota(I32, (IDX_DIM, LANES), 0)
    c_i = lax.broadcasted_iota(I32, (IDX_DIM, LANES), 1)
    place = jnp.where(c_i == bl * t_new + lax.rem(r_i, t_new), 1.0, 0.0).astype(BF16)
    placed = _dot(hi, place) + _dot(lo, place)

    @pl.when(bl == 0)
    def _():
        o_ref[0] = placed

    @pl.when(bl != 0)
    def _():
        o_ref[0] = o_ref[0] + placed


def _sample_select_kernel(sc_ref, sel_ref, key_ref, j_ref, *, past, t_new, n_keep, idx_bits):
    s_pad = sc_ref.shape[1]
    row = lax.broadcasted_iota(I32, (s_pad, LANES), 0)
    lane = lax.broadcasted_iota(I32, (s_pad, LANES), 1)
    valid = row <= past + lax.rem(lane, t_new)
    key_ref[...] = _sortable_key(jnp.where(valid, sc_ref[0], NEG_INF))
    kf = float(n_keep)

    def count(weight):
        return jnp.sum(weight(key_ref[...]), axis=0, keepdims=True)

    c0 = count(lambda kk: jnp.where(kk >= 0, 1.0, 0.0))
    thr = jnp.where(c0 >= kf, jnp.zeros((1, LANES), I32), jnp.full((1, LANES), INT_MIN, I32))

    def p2(it, thr):
        cand = thr + jnp.left_shift(jnp.int32(1), 30 - it)
        cnt = count(lambda kk: jnp.where(kk >= cand, 1.0, 0.0))
        return jnp.where(cnt >= kf, cand, thr)

    thr = lax.fori_loop(0, 31, p2, thr)
    cnt_ge = count(lambda kk: jnp.where(kk >= thr, 1.0, 0.0))
    need = kf - count(lambda kk: jnp.where(kk > thr, 1.0, 0.0))
    j_ref[...] = jnp.full(j_ref.shape, INT_MAX, I32)

    @pl.when(jnp.max(cnt_ge) > kf)
    def _():
        def p3(it, jj):
            cand = jj + jnp.left_shift(jnp.int32(1), idx_bits - 1 - it)
            cnt = count(lambda kk: jnp.where(kk == thr, jnp.where(row < cand, 1.0, 0.0), 0.0))
            return jnp.where(cnt < need, cand, jj)
        jj = lax.fori_loop(0, idx_bits, p3, jnp.zeros((1, LANES), I32))
        j_ref[0:1, :] = jnp.where(cnt_ge > kf, jj, INT_MAX)

    jlast = j_ref[0:1, :]
    kk = key_ref[...]
    tie = jnp.where(kk == thr, jnp.where(row <= jlast, 1.0, 0.0), 0.0)
    sel_ref[0] = jnp.where(valid, jnp.where(kk > thr, 1.0, tie), 0.0).astype(sel_ref.dtype)


def _sample_attend_kernel(pt_ref, q_ref, kn_ref, vn_ref, sel_ref, *rest, n_pages, t_new, group):
    kpages = rest[:n_pages]
    vpages = rest[n_pages:2 * n_pages]
    o_ref, kall, vall = rest[2 * n_pages:]
    bl = lax.rem(pl.program_id(0), group)
    page = kpages[0].shape[1]
    past = n_pages * page
    s_pad, dk = kall.shape
    for p in range(n_pages):
        rows = slice(p * page, (p + 1) * page)
        kall[rows, :] = kpages[p][0].astype(BF16)
        vall[rows, 0:dk] = vpages[p][0].astype(BF16)
    kall[past:s_pad, :] = _with_zero_tail(kn_ref[...], page).astype(BF16)
    vall[past:s_pad, 0:dk] = _with_zero_tail(vn_ref[...], page).astype(BF16)
    vall[:, dk:dk + LANES] = jnp.where(lax.broadcasted_iota(I32, (s_pad, LANES), 1) == 0, 1.0, 0.0).astype(BF16)

    qf = q_ref[...].astype(F32)[0:t_new] * (HEAD_DIM ** -0.5 * math.log2(math.e))
    lo = lax.broadcasted_iota(I32, (t_new, LANES), 1) < HEAD_DIM
    zero = jnp.zeros((t_new, LANES), F32)
    per_group = N_HEADS // N_KV_HEADS
    qrows = []
    for h in range(N_HEADS):
        g = h // per_group
        blk = qf[:, (h // 2) * LANES:(h // 2 + 1) * LANES]
        if h % 2 != g % 2:
            blk = pltpu.roll(blk, HEAD_DIM, axis=1)
        blk = jnp.where(lo, blk, zero) if g % 2 == 0 else jnp.where(lo, zero, blk)
        qrows.append(jnp.concatenate([blk, zero] if g // 2 == 0 else [zero, blk], axis=1))
    qbd = jnp.concatenate(qrows, axis=0).astype(BF16)
    s_t = _dot_nt(kall[...], qbd)

    r_i = lax.broadcasted_iota(I32, (LANES, LANES), 0)
    c_i = lax.broadcasted_iota(I32, (LANES, LANES), 1)
    widen = jnp.where(r_i == bl * t_new + lax.rem(c_i, t_new), 1.0, 0.0).astype(BF16)
    s_t = jnp.where(_dot(sel_ref[0], widen) > 0.5, s_t, NEG_INF)
    p = jnp.exp2(s_t - jnp.max(s_t, axis=0, keepdims=True)).astype(BF16)
    o_all = lax.dot_general(p, vall[...], (((0,), (0,)), ((), ())), preferred_element_type=F32)
    o = o_all[:, 0:dk] / o_all[:, dk:dk + 1]

    outs = []
    for m in range(N_HEADS // 2):
        pair = []
        for h in (2 * m, 2 * m + 1):
            g = h // per_group
            piece = o[h * t_new:(h + 1) * t_new, (g // 2) * LANES:(g // 2 + 1) * LANES]
            pair.append(pltpu.roll(piece, HEAD_DIM, axis=1) if g % 2 != h % 2 else piece)
        outs.append(jnp.where(lo, pair[0], pair[1]))
    o_ref[...] = jnp.concatenate(outs, axis=1)


def _attn_sample(q, iq, iw_t, k_new, v_new, ik_new, cache_k, cache_v, cache_ik, page_table, t_new):
    nb, n_pages = page_table.shape
    n_pool, page, dk = cache_k.shape
    past, s_pad, group = _sample_geometry(n_pages, page, t_new)
    assert nb % group == 0
    n_groups = nb // group
    n_keep = min(TOPK_KEYS, (past + t_new) // 4)
    dq = q.shape[1]
    row = lambda b, pt: (b, 0)
    grp = lambda b, pt: (b // group, 0, 0)

    def page_spec(width, p):
        return pl.BlockSpec((1, page, width), lambda b, pt, p=p: (pt[b, p], 0, 0))

    scores = pl.pallas_call(
        functools.partial(_sample_scores_kernel, n_pages=n_pages, t_new=t_new, group=group),
        grid_spec=pltpu.PrefetchScalarGridSpec(
            num_scalar_prefetch=1,
            grid=(nb,),
            in_specs=[pl.BlockSpec((SAMPLE_Q_ROWS, iq.shape[1]), row),
                      pl.BlockSpec((1, 1, iw_t.shape[2]), lambda b, pt: (b, 0, 0)),
                      pl.BlockSpec((t_new, IDX_DIM), row)] + [page_spec(IDX_DIM, p) for p in range(n_pages)],
            out_specs=pl.BlockSpec((1, s_pad, LANES), grp),
            scratch_shapes=[pltpu.VMEM((s_pad, IDX_DIM), BF16)]),
        out_shape=jax.ShapeDtypeStruct((n_groups, s_pad, LANES), F32),
        compiler_params=_cparams(("arbitrary",)),
        name="sample_scores",
    )(page_table, iq, iw_t, ik_new, *([cache_ik] * n_pages))

    sel = pl.pallas_call(
        functools.partial(_sample_select_kernel, past=past, t_new=t_new, n_keep=n_keep,
                          idx_bits=max(1, math.ceil(math.log2(s_pad)))),
        grid=(n_groups,),
        in_specs=[pl.BlockSpec((1, s_pad, LANES), lambda g: (g, 0, 0))],
        out_specs=pl.BlockSpec((1, s_pad, LANES), lambda g: (g, 0, 0)),
        out_shape=jax.ShapeDtypeStruct((n_groups, s_pad, LANES), BF16),
        scratch_shapes=[pltpu.VMEM((s_pad, LANES), I32), pltpu.VMEM((SUBLANES, LANES), I32)],
        compiler_params=_cparams(("parallel",)),
        name="sample_select",
    )(scores)

    return pl.pallas_call(
        functools.partial(_sample_attend_kernel, n_pages=n_pages, t_new=t_new, group=group),
        grid_spec=pltpu.PrefetchScalarGridSpec(
            num_scalar_prefetch=1,
            grid=(nb,),
            in_specs=[pl.BlockSpec((SAMPLE_Q_ROWS, dq), row),
                      pl.BlockSpec((t_new, dk), row), pl.BlockSpec((t_new, dk), row),
                      pl.BlockSpec((1, s_pad, LANES), grp)]
            + [page_spec(dk, p) for p in range(n_pages)] + [page_spec(dk, p) for p in range(n_pages)],
            out_specs=pl.BlockSpec((t_new, dq), row),
            scratch_shapes=[pltpu.VMEM((s_pad, dk), BF16), pltpu.VMEM((s_pad, dk + LANES), BF16)]),
        out_shape=jax.ShapeDtypeStruct((nb * t_new, dq), F32),
        compiler_params=_cparams(("parallel",)),
        name="sample_attend",
    )(page_table, q, k_new, v_new, sel, *([cache_k] * n_pages), *([cache_v] * n_pages))


def _layer_norm(x, g, b):
    mu = jnp.mean(x, axis=-1, keepdims=True)
    xc = x - mu
    var = jnp.mean(xc * xc, axis=-1, keepdims=True)
    return xc * lax.rsqrt(var + LN_EPS) * g + b


def _merge_kernel(yr_ref, ya_ref, ga_ref, gb_ref, x_ref, wa_ref, wb_ref, wo_ref, g_ref, b_ref, h_ref,
                  *, alpha):
    pa = _dot(yr_ref[...], wa_ref[...])
    pb = _dot(ya_ref[...], wb_ref[...])
    m = jax.nn.sigmoid(ga_ref[...]) * pa + jax.nn.sigmoid(gb_ref[...]) * pb
    mix = _dot(m.astype(BF16), wo_ref[...])
    h_ref[...] = _layer_norm(alpha * x_ref[...] + mix, g_ref[...], b_ref[...])


def _merge(y_rnn, y_att, ga, gb, x2d, wbr_a, wbr_b, w_out, ln_g, ln_b, alpha, tm):
    n, d = x2d.shape
    row = lambda i: (i, 0)
    full = lambda i: (0, 0)
    return pl.pallas_call(
        functools.partial(_merge_kernel, alpha=alpha),
        grid=(n // tm,),
        in_specs=[pl.BlockSpec((tm, d), row)] * 5 + [pl.BlockSpec((d, d), full)] * 3
        + [pl.BlockSpec((1, d), full)] * 2,
        out_specs=pl.BlockSpec((tm, d), row),
        out_shape=jax.ShapeDtypeStruct((n, d), F32),
        compiler_params=_cparams(("parallel",)),
        name="merge_ln1",
    )(y_rnn, y_att, ga, gb, x2d, wbr_a, wbr_b, w_out, ln_g, ln_b)


def _router_kernel(h_ref, whi_ref, wlo_ref, bias_ref, e_ref, g_ref, r_ref, cnt_ref, run_ref):
    h = h_ref[...]
    tm = h.shape[0]
    ne = whi_ref.shape[1]
    h_hi = h.astype(BF16)
    h_lo = (h - h_hi.astype(F32)).astype(BF16)
    logits = _dot(h_hi, whi_ref[...]) + (_dot(h_lo, whi_ref[...]) + _dot(h_hi, wlo_ref[...]))
    scores = jax.nn.sigmoid(logits)
    biased = scores + bias_ref[...]
    lane = lax.broadcasted_iota(I32, (tm, ne), 1)
    lane_f = lane.astype(F32)
    per_group = ne // N_EXPERT_GROUPS
    big = float(ne)

    def first_argmax(v):
        m = jnp.max(v, axis=1, keepdims=True)
        idx = jnp.min(jnp.where(v == m, lane_f, big), axis=1, keepdims=True)
        return m, idx

    gscore = []
    for g in range(N_EXPERT_GROUPS):
        in_g = (lane >= g * per_group) & (lane < (g + 1) * per_group)
        mg = jnp.where(in_g, biased, NEG_INF)
        m1, i1 = first_argmax(mg)
        m2 = jnp.max(jnp.where(lane_f == i1, NEG_INF, mg), axis=1, keepdims=True)
        gscore.append(m1 + m2)

    ok_map = jnp.zeros((tm, ne), F32)
    for g in range(N_EXPERT_GROUPS):
        rank = jnp.zeros((tm, 1), F32)
        for o in range(N_EXPERT_GROUPS):
            if o == g:
                continue
            ahead = (gscore[o] > gscore[g]) if o > g else (gscore[o] >= gscore[g])
            rank = rank + jnp.where(ahead, 1.0, 0.0)
        in_g = (lane >= g * per_group) & (lane < (g + 1) * per_group)
        ok_map = jnp.where(in_g, jnp.where(rank < float(TOPK_GROUPS), 1.0, 0.0), ok_map)

    cur = jnp.where(ok_map > 0.5, biased, NEG_INF)
    out_lane = lax.broadcasted_iota(I32, (tm, LANES), 1)
    e_out = jnp.zeros((tm, LANES), F32)
    s_out = jnp.zeros((tm, LANES), F32)
    total = jnp.zeros((tm, 1), F32)
    picked = jnp.zeros((tm, ne), F32)
    hits = []
    for j in range(EXPERT_TOP_K):
        _, idx = first_argmax(cur)
        hit = lane_f == idx
        hits.append(hit)
        sel = jnp.sum(jnp.where(hit, scores, 0.0), axis=1, keepdims=True)
        cur = jnp.where(hit, NEG_INF, cur)
        picked = jnp.where(hit, 1.0, picked)
        e_out = jnp.where(out_lane == j, idx, e_out)
        s_out = jnp.where(out_lane == j, sel, s_out)
        total = total + sel
    e_ref[...] = e_out
    g_ref[...] = ROUTED_SCALE * s_out / total

    @pl.when(pl.program_id(0) == 0)
    def _():
        run_ref[...] = jnp.zeros_like(run_ref)

    pk = picked.astype(BF16)
    r_i = lax.broadcasted_iota(I32, (tm, tm), 0)
    c_i = lax.broadcasted_iota(I32, (tm, tm), 1)
    before = _dot(jnp.where(c_i < r_i, 1.0, 0.0).astype(BF16), pk) + run_ref[0:1, :]
    r_out = jnp.zeros((tm, LANES), F32)
    for j in range(EXPERT_TOP_K):
        rank = jnp.sum(jnp.where(hits[j], before, 0.0), axis=1, keepdims=True)
        r_out = jnp.where(out_lane == j, rank, r_out)
    r_ref[...] = r_out
    run_ref[...] = run_ref[...] + _dot(jnp.ones((SUBLANES, tm), BF16), pk)
    cnt_ref[...] = run_ref[...]


def _router(h, w_hi, w_lo, bias, tm):
    n, d = h.shape
    ne = w_hi.shape[1]
    row = lambda i: (i, 0)
    full = lambda i: (0, 0)
    return pl.pallas_call(
        _router_kernel,
        grid=(n // tm,),
        in_specs=[pl.BlockSpec((tm, d), row), pl.BlockSpec((d, ne), full), pl.BlockSpec((d, ne), full),
                  pl.BlockSpec((1, ne), full)],
        out_specs=(pl.BlockSpec((tm, LANES), row), pl.BlockSpec((tm, LANES), row),
                   pl.BlockSpec((tm, LANES), row), pl.BlockSpec((SUBLANES, ne), full)),
        out_shape=(jax.ShapeDtypeStruct((n, LANES), F32), jax.ShapeDtypeStruct((n, LANES), F32),
                   jax.ShapeDtypeStruct((n, LANES), F32), jax.ShapeDtypeStruct((SUBLANES, ne), F32)),
        scratch_shapes=[pltpu.VMEM((SUBLANES, ne), F32)],
        compiler_params=_cparams(("arbitrary",)),
        name="router",
    )(h, w_hi, w_lo, bias)


def _row_layout(counts, n_rows):
    n_experts = counts.shape[0]
    n_blocks = n_rows // MOE_BLOCK
    padded = (counts + MOE_BLOCK - 1) // MOE_BLOCK * MOE_BLOCK
    pad_end = jnp.cumsum(padded)
    pad_start = pad_end - padded
    block_start = jnp.arange(n_blocks, dtype=I32) * MOE_BLOCK
    block_expert = jnp.minimum(jnp.searchsorted(pad_end, block_start, side="right"), n_experts - 1).astype(I32)
    n_active = (pad_end[-1] // MOE_BLOCK).astype(I32).reshape(1)
    return pad_start.astype(I32), pad_end.astype(I32), block_expert, n_active


def _dest_kernel(e_ref, r_ref, ps_ref, d_ref):
    e = e_ref[...]
    tm = e.shape[0]
    ne = ps_ref.shape[1]
    lane_f = lax.broadcasted_iota(I32, (tm, ne), 1).astype(F32)
    out_lane = lax.broadcasted_iota(I32, (tm, LANES), 1)
    ps = ps_ref[...]
    out = r_ref[...]
    for j in range(EXPERT_TOP_K):
        start = jnp.sum(jnp.where(lane_f == e[:, j:j + 1], ps, 0.0), axis=1, keepdims=True)
        out = jnp.where(out_lane == j, out + start, out)
    d_ref[...] = out.astype(I32)


def _dest(e_f, r_f, pad_start_f, tm):
    n = e_f.shape[0]
    ne = pad_start_f.shape[1]
    row = lambda i: (i, 0)
    return pl.pallas_call(
        _dest_kernel,
        grid=(n // tm,),
        in_specs=[pl.BlockSpec((tm, LANES), row), pl.BlockSpec((tm, LANES), row),
                  pl.BlockSpec((1, ne), lambda i: (0, 0))],
        out_specs=pl.BlockSpec((tm, LANES), row),
        out_shape=jax.ShapeDtypeStruct((n, LANES), I32),
        compiler_params=_cparams(("parallel",)),
        name="moe_dest",
    )(e_f, r_f, pad_start_f)


def _pow2_below(n):
    return [1 << b for b in range(n.bit_length() - 1, -1, -1)]


def _scatter_kernel(cnt_ref, ps_ref, pe_ref, dest_ref, h_ref, xs_hbm, zbuf, sem, zsem, *, top_k, n_rows):
    i = pl.program_id(0)
    tm = h_ref.shape[0]
    n_experts = cnt_ref.shape[0]

    def row_copy(t, j):
        return pltpu.make_async_copy(h_ref.at[pl.ds(t, 1), :],
                                     xs_hbm.at[pl.ds(dest_ref[0, 0, t * top_k + j], 1), :], sem)

    def issue(t, carry):
        for j in range(top_k):
            row_copy(t, j).start()
        return carry

    lax.fori_loop(0, tm, issue, 0)

    def zero_copy(start, size):
        return pltpu.make_async_copy(zbuf.at[pl.ds(0, size), :], xs_hbm.at[pl.ds(start, size), :], zsem)

    @pl.when(i == 0)
    def _():
        zbuf[...] = jnp.zeros_like(zbuf)

        def fill(first, end, wait):
            aligned = (first + (SUBLANES - 1)) & ~(SUBLANES - 1)
            for s in range(SUBLANES - 1):
                @pl.when(first + s < aligned)
                def _():
                    cp = zero_copy(first + s, 1)
                    cp.wait() if wait else cp.start()
            n_tiles = lax.div(end - aligned, jnp.int32(SUBLANES))
            for size in _pow2_below(MOE_BLOCK // SUBLANES - 1):
                @pl.when((n_tiles & size) != 0)
                def _():
                    start = aligned + (n_tiles & ~(2 * size - 1)) * SUBLANES
                    cp = zero_copy(pl.multiple_of(start, SUBLANES), size * SUBLANES)
                    cp.wait() if wait else cp.start()

        def per_expert(wait):
            def body(e, carry):
                fill(ps_ref[e] + cnt_ref[e], pe_ref[e], wait)
                return carry
            lax.fori_loop(0, n_experts, body, 0)

        def tail(wait):
            total = pe_ref[n_experts - 1]

            def body(b, carry):
                cp = zero_copy(pl.multiple_of(total + b * MOE_BLOCK, MOE_BLOCK), MOE_BLOCK)
                cp.wait() if wait else cp.start()
                return carry
            lax.fori_loop(0, lax.div(n_rows - total, jnp.int32(MOE_BLOCK)), body, 0)

        per_expert(False)
        tail(False)
        per_expert(True)
        tail(True)

    pltpu.make_async_copy(h_ref, xs_hbm.at[pl.ds(0, tm), :], sem).wait()
    for _ in range(top_k - 1):
        pltpu.make_async_copy(h_ref, xs_hbm.at[pl.ds(0, tm), :], sem).wait()


def _scatter_rows(h, dest, counts, pad_start, pad_end, n_rows, top_k, tm):
    n, d = h.shape
    grid_spec = pltpu.PrefetchScalarGridSpec(
        num_scalar_prefetch=3,
        grid=(n // tm,),
        in_specs=[pl.BlockSpec((1, 1, tm * top_k), lambda i, *_: (i, 0, 0), memory_space=pltpu.SMEM),
                  pl.BlockSpec((tm, d), lambda i, *_: (i, 0))],
        out_specs=pl.BlockSpec(memory_space=pl.ANY),
        scratch_shapes=[pltpu.VMEM((MOE_BLOCK, d), F32), pltpu.SemaphoreType.DMA(()),
                        pltpu.SemaphoreType.DMA(())])
    return pl.pallas_call(
        functools.partial(_scatter_kernel, top_k=top_k, n_rows=n_rows),
        grid_spec=grid_spec,
        out_shape=jax.ShapeDtypeStruct((n_rows, d), F32),
        compiler_params=_cparams(("arbitrary",)),
        name="moe_scatter",
    )(counts, pad_start, pad_end, dest.reshape(n // tm, 1, tm * top_k), h)


def _experts_kernel(be_ref, na_ref, x_ref, wg_ref, wu_ref, wd_ref, o_ref):
    i = pl.program_id(0)

    @pl.when(i < na_ref[0])
    def _():
        xb = x_ref[...].astype(BF16)
        g = _dot(xb, wg_ref[0].astype(BF16))
        u = _dot(xb, wu_ref[0].astype(BF16))
        hb = (g * jax.nn.sigmoid(g)) * u
        o_ref[...] = _dot(hb.astype(BF16), wd_ref[0].astype(BF16))

    @pl.when(i >= na_ref[0])
    def _():
        o_ref[...] = jnp.zeros(o_ref.shape, o_ref.dtype)


def _experts(xs, block_expert, n_active, w_gate, w_up, w_down):
    n_rows, d = xs.shape
    ne, _, de = w_gate.shape
    n_blocks = n_rows // MOE_BLOCK
    live = lambda i, be, na: (jnp.minimum(i, na[0] - 1), 0)
    wsel = lambda i, be, na: (be[jnp.minimum(i, na[0] - 1)], 0, 0)
    grid_spec = pltpu.PrefetchScalarGridSpec(
        num_scalar_prefetch=2,
        grid=(n_blocks,),
        in_specs=[pl.BlockSpec((MOE_BLOCK, d), live),
                  pl.BlockSpec((1, d, de), wsel), pl.BlockSpec((1, d, de), wsel), pl.BlockSpec((1, de, d), wsel)],
        out_specs=pl.BlockSpec((MOE_BLOCK, d), lambda i, be, na: (i, 0)))
    return pl.pallas_call(
        _experts_kernel,
        grid_spec=grid_spec,
        out_shape=jax.ShapeDtypeStruct((n_rows, d), F32),
        compiler_params=_cparams(("arbitrary",)),
        name="moe_experts",
    )(block_expert, n_active, xs, w_gate, w_up, w_down)


def _combine_kernel(dest_ref, rows_hbm, h_ref, gate_ref, wsg_ref, wsu_ref, wsd_ref, g_ref, b_ref, y_ref,
                    gbuf, sem, *, alpha, top_k):
    tm = h_ref.shape[0]

    def issue(t, carry):
        for j in range(top_k):
            pltpu.make_async_copy(rows_hbm.at[pl.ds(dest_ref[0, 0, t * top_k + j], 1), :],
                                  gbuf.at[pl.ds(j * tm + t, 1), :], sem).start()
        return carry

    lax.fori_loop(0, tm, issue, 0)
    h = h_ref[...]
    hb = h.astype(BF16)
    sg = _dot(hb, wsg_ref[...])
    su = _dot(hb, wsu_ref[...])
    shared = _dot(((sg * jax.nn.sigmoid(sg)) * su).astype(BF16), wsd_ref[...])
    gates = gate_ref[...]
    pltpu.make_async_copy(rows_hbm.at[pl.ds(0, tm * top_k), :], gbuf, sem).wait()
    routed = gbuf[0:tm, :] * gates[:, 0:1]
    for j in range(1, top_k):
        routed = routed + gbuf[j * tm:(j + 1) * tm, :] * gates[:, j:j + 1]
    y_ref[...] = _layer_norm(alpha * h + (routed + shared), g_ref[...], b_ref[...])


def _combine(h, rows, dest, gates, wsg, wsu, wsd, ln_g, ln_b, alpha, top_k, tm):
    n, d = h.shape
    ds_ = wsg.shape[1]
    row = lambda i: (i, 0)
    full = lambda i: (0, 0)
    return pl.pallas_call(
        functools.partial(_combine_kernel, alpha=alpha, top_k=top_k),
        grid=(n // tm,),
        in_specs=[pl.BlockSpec((1, 1, tm * top_k), lambda i: (i, 0, 0), memory_space=pltpu.SMEM),
                  pl.BlockSpec(memory_space=pl.ANY),
                  pl.BlockSpec((tm, d), row), pl.BlockSpec((tm, LANES), row),
                  pl.BlockSpec((d, ds_), full), pl.BlockSpec((d, ds_), full), pl.BlockSpec((ds_, d), full),
                  pl.BlockSpec((1, d), full), pl.BlockSpec((1, d), full)],
        out_specs=pl.BlockSpec((tm, d), row),
        out_shape=jax.ShapeDtypeStruct((n, d), F32),
        scratch_shapes=[pltpu.VMEM((tm * top_k, d), F32), pltpu.SemaphoreType.DMA(())],
        compiler_params=_cparams(("arbitrary",)),
        name="moe_combine",
    )(dest.reshape(n // tm, 1, tm * top_k), rows, h, gates, wsg, wsu, wsd, ln_g, ln_b)


def _moe(h, w_r_hi, w_r_lo, r_bias, w_gate, w_up, w_down, wsg, wsu, wsd, ln_g, ln_b, alpha):
    n, d = h.shape
    ne = w_r_hi.shape[1]
    top_k = EXPERT_TOP_K
    n_rows = -(-(n * top_k + ne * (MOE_BLOCK - 1)) // MOE_BLOCK) * MOE_BLOCK
    e_f, g_f, r_f, cnt = _router(h, w_r_hi, w_r_lo, r_bias, 128)
    counts = cnt[0].astype(I32)
    pad_start, pad_end, block_expert, n_active = _row_layout(counts, n_rows)
    dest = _dest(e_f, r_f, pad_start.astype(F32).reshape(1, ne), 256)[:, :top_k]
    xs = _scatter_rows(h, dest, counts, pad_start, pad_end, n_rows, top_k, 128)
    ys = _experts(xs, block_expert, n_active, w_gate, w_up, w_down)
    return _combine(h, ys, dest, g_f, wsg, wsu, wsd, ln_g, ln_b, alpha, top_k, 128)


def _layer(xp, xs, cache_k, cache_v, cache_idx_k, state_conv, state_h, page_table, depth,
           w_in, conv_w, conv_b, w_a, b_a, w_i, b_i, lru_lambda, w_branch, w_out, ln1_g, ln1_b,
           w_router, router_bias, w_exp_gate, w_exp_up, w_exp_down, w_sh_gate, w_sh_up, w_sh_down,
           ln2_g, ln2_b):
    bp, tp, d = xp.shape
    bs, ts, _ = xs.shape
    n_pages = page_table.shape[1]
    past = n_pages * PAGE_SIZE
    alpha = (2.0 * depth) ** 0.25
    d_rnn = conv_w.shape[1]
    dkv = N_KV_HEADS * HEAD_DIM

    w_packed = _pack_w_in(w_in)
    lru_w = _prep_lru(conv_w, conv_b, w_a, b_a, w_i, b_i, lru_lambda)

    np_ = bp * tp
    tabs_p = _rope_tables(jnp.arange(tp, dtype=I32))
    tm = 256
    (xr, gr, q, k_p, v_p, iq, ik_p, ikb, iw, ga_p, gb_p) = _project(
        xp.reshape(np_, d), w_packed, tabs_p, tp // tm, tm)
    y_rnn_p, h_p = _rglru_prompt(xr, gr, bp, tp, lru_w, 256)
    conv_p = xr.reshape(bp, tp, d_rnn)[:, tp - (CONV_W - 1):, :]
    y_att_p = _attn_prompt(q, iq, iw, k_p, v_p, ikb, bp, tp, 128, 512)

    ns = bs * ts
    pos_s = past + jnp.arange(ts, dtype=I32)
    tabs_s = tuple(jnp.tile(t_, (ns // ts, 1)) for t_ in _rope_tables(pos_s))
    (xr_s, gr_s, q_s, k_s, v_s, iq_s, ik_s, _, iw_s, ga_s, gb_s) = _project(
        xs.reshape(ns, d), w_packed, tabs_s, ns // tm, tm)
    xp_s = jnp.concatenate([state_conv.astype(F32), xr_s.reshape(bs, ts, d_rnn)], axis=1)
    y_rnn_s, hs_s = _rglru_sample(xp_s, gr_s.reshape(bs, ts, d_rnn), state_h.reshape(bs, 1, d_rnn), lru_w, 16)
    conv_s = xp_s[:, ts:, :]
    h_s = hs_s[:, ts - 1, :]

    def pad_q(a):
        a3 = a.reshape(bs, ts, a.shape[-1])
        return jnp.pad(a3, ((0, 0), (0, SAMPLE_Q_ROWS - ts), (0, 0))).reshape(bs * SAMPLE_Q_ROWS, a.shape[-1])

    n_pool = cache_k.shape[0]
    iw_t = iw_s.reshape(bs, ts, IDX_HEADS).transpose(0, 2, 1).reshape(bs, 1, IDX_HEADS * ts)
    y_att_s = _attn_sample(pad_q(q_s), pad_q(iq_s), iw_t, k_s, v_s, ik_s,
                           cache_k.reshape(n_pool, PAGE_SIZE, dkv), cache_v.reshape(n_pool, PAGE_SIZE, dkv),
                           cache_idx_k, page_table, ts).astype(BF16)

    cat = lambda a, b_: jnp.concatenate([a, b_], axis=0)
    wbr = w_branch.astype(BF16)
    h1 = _merge(cat(y_rnn_p, y_rnn_s.reshape(ns, d_rnn)), cat(y_att_p, y_att_s), cat(ga_p, ga_s), cat(gb_p, gb_s),
                cat(xp.reshape(np_, d), xs.reshape(ns, d)), wbr[:d_rnn], wbr[d_rnn:], w_out.astype(BF16),
                ln1_g.reshape(1, d), ln1_b.reshape(1, d), alpha, 256)

    w_r = w_router.astype(F32)
    w_r_hi = w_r.astype(BF16)
    w_r_lo = (w_r - w_r_hi.astype(F32)).astype(BF16)
    ne = w_router.shape[1]
    y = _moe(h1, w_r_hi, w_r_lo, router_bias.reshape(1, ne).astype(F32), w_exp_gate, w_exp_up, w_exp_down,
             w_sh_gate.astype(BF16), w_sh_up.astype(BF16), w_sh_down.astype(BF16),
             ln2_g.reshape(1, d), ln2_b.reshape(1, d), alpha)

    yp = y[:np_].reshape(bp, tp, d)
    ys = y[np_:].reshape(bs, ts, d)
    st = (k_p.reshape(bp, tp, N_KV_HEADS, HEAD_DIM), v_p.reshape(bp, tp, N_KV_HEADS, HEAD_DIM),
          ik_p.reshape(bp, tp, IDX_DIM), conv_p, h_p,
          k_s.reshape(bs, ts, N_KV_HEADS, HEAD_DIM), v_s.reshape(bs, ts, N_KV_HEADS, HEAD_DIM),
          ik_s.reshape(bs, ts, IDX_DIM), conv_s, h_s)
    return yp, ys, st


def kernel(x_prompt, x_sample, cache_k, cache_v, cache_idx_k, state_conv, state_h, page_table, w_in, conv_w,
           conv_b, w_a, b_a, w_i, b_i, lru_lambda, w_branch, w_out, ln1_g, ln1_b, w_router, router_bias,
           w_exp_gate, w_exp_up, w_exp_down, w_sh_gate, w_sh_up, w_sh_down, ln2_g, ln2_b):
    depth = w_in.shape[0]
    yp, ys = x_prompt, x_sample
    states = []
    for l in range(depth):
        yp, ys, st = _layer(
            yp, ys, cache_k[l], cache_v[l], cache_idx_k[l], state_conv[l], state_h[l], page_table, depth,
            w_in[l], conv_w[l], conv_b[l], w_a[l], b_a[l], w_i[l], b_i[l], lru_lambda[l], w_branch[l], w_out[l],
            ln1_g[l], ln1_b[l], w_router[l], router_bias[l], w_exp_gate[l], w_exp_up[l], w_exp_down[l],
            w_sh_gate[l], w_sh_up[l], w_sh_down[l], ln2_g[l], ln2_b[l])
        states.append(st)
    stacked = [jnp.stack(c) for c in zip(*states)]
    return (yp, ys, *stacked)
```

```python
import functools
import math

import jax
import jax.numpy as jnp
from jax import lax
from jax.experimental import pallas as pl
from jax.experimental.pallas import tpu as pltpu

F32 = jnp.float32
BF16 = jnp.bfloat16
I32 = jnp.int32

LRU_BLOCKS = 16
CONV_W = 4
LRU_C = 8.0
N_HEADS = 16
N_KV_HEADS = 4
HEAD_DIM = 64
ROT_DIM = 16
ROPE_THETA = 500000.0
IDX_HEADS = 8
IDX_DIM = 64
TOPK_KEYS = 256
PAGE_SIZE = 128
N_EXPERT_GROUPS = 8
TOPK_GROUPS = 4
EXPERT_TOP_K = 8
ROUTED_SCALE = 2.5
MOE_BLOCK = 128
LN_EPS = 1e-5

LANES = 128
SUBLANES = 8
VMEM_LIMIT = 56 * 1024 * 1024

NEG_INF = float("-inf")


def _cparams(sem):
    return pltpu.CompilerParams(dimension_semantics=sem, vmem_limit_bytes=VMEM_LIMIT)


def _dot(a, b):
    return jnp.dot(a, b, preferred_element_type=F32)


def _dot_nt(a, b):
    return lax.dot_general(a, b, (((1,), (1,)), ((), ())), preferred_element_type=F32)


_PROJ_GROUPS = (
    ("xr", 1024, 1024, False),
    ("gr", 1024, 1024, False),
    ("q", 1024, 1024, True),
    ("k", 256, 256, True),
    ("v", 256, 256, False),
    ("iq", 512, 512, True),
    ("ik", 64, 128, True),
    ("iw", 8, 128, False),
    ("ga", 1024, 1024, False),
    ("gb", 1024, 1024, False),
)


def _pack_w_in(w_in):
    cols = []
    c0 = 0
    for _, w, wp, _ in _PROJ_GROUPS:
        blk = w_in[:, c0:c0 + w]
        if wp != w:
            blk = jnp.pad(blk, ((0, 0), (0, wp - w)))
        cols.append(blk)
        c0 += w
    return jnp.concatenate(cols, axis=1).astype(BF16)


def _rope_tables(pos):
    half = ROT_DIM // 2
    inv_freq = jnp.power(ROPE_THETA, -jnp.arange(half, dtype=F32) * (2.0 / ROT_DIM))
    ang = pos.astype(F32)[:, None] * inv_freq[None, :]
    cos = jnp.cos(ang)
    sin = jnp.sin(ang)
    t = pos.shape[0]
    rest = HEAD_DIM - ROT_DIM
    c = jnp.concatenate([cos, cos, jnp.ones((t, rest), F32)], axis=1)
    sa = jnp.concatenate([-sin, jnp.zeros((t, half + rest), F32)], axis=1)
    sb = jnp.concatenate([jnp.zeros((t, half), F32), sin, jnp.zeros((t, rest), F32)], axis=1)
    rep = LANES // HEAD_DIM
    return jnp.tile(c, (1, rep)), jnp.tile(sa, (1, rep)), jnp.tile(sb, (1, rep))


def _proj_kernel(x_ref, w_ref, c_ref, sa_ref, sb_ref,
                 xr_ref, gr_ref, q_ref, k_ref, v_ref, iq_ref,
                 ik_ref, ikb_ref, iw_ref, ga_ref, gb_ref, *, iw_scale):
    xb = x_ref[...].astype(BF16)
    c = c_ref[...]
    sa = sa_ref[...]
    sb = sb_ref[...]

    def rope(z):
        n = z.shape[1] // LANES
        cc = jnp.concatenate([c] * n, axis=1) if n > 1 else c
        aa = jnp.concatenate([sa] * n, axis=1) if n > 1 else sa
        bb = jnp.concatenate([sb] * n, axis=1) if n > 1 else sb
        half = ROT_DIM // 2
        up = pltpu.roll(z, z.shape[1] - half, axis=1)
        dn = pltpu.roll(z, half, axis=1)
        return z * cc + up * aa + dn * bb

    c0 = 0
    zs = {}
    for name, _, wp, rot in _PROJ_GROUPS:
        z = _dot(xb, w_ref[:, c0:c0 + wp])
        zs[name] = rope(z) if rot else z
        c0 += wp
    xr_ref[...] = zs["xr"]
    gr_ref[...] = zs["gr"]
    q_ref[...] = zs["q"].astype(BF16)
    k_ref[...] = zs["k"]
    v_ref[...] = zs["v"]
    iq_ref[...] = zs["iq"].astype(BF16)
    ik_ref[...] = zs["ik"][:, :IDX_DIM]
    ikb_ref[...] = zs["ik"].astype(BF16)
    iw_ref[...] = zs["iw"][:, :IDX_HEADS] * iw_scale
    ga_ref[...] = zs["ga"]
    gb_ref[...] = zs["gb"]


def _project(x2d, w_packed, tabs, n_tab_blocks, tm):
    m, d = x2d.shape
    n_total = w_packed.shape[1]
    grid = (m // tm,)
    row = lambda i: (i, 0)
    tab = lambda i: (i % n_tab_blocks, 0)
    out_defs = (
        (1024, F32), (1024, F32), (1024, BF16), (256, F32), (256, F32),
        (512, BF16), (IDX_DIM, F32), (LANES, BF16), (IDX_HEADS, F32), (1024, F32), (1024, F32))
    out_shape = tuple(jax.ShapeDtypeStruct((m, w), dt) for w, dt in out_defs)
    out_specs = tuple(pl.BlockSpec((tm, w), row) for w, _ in out_defs)
    return pl.pallas_call(
        functools.partial(_proj_kernel, iw_scale=IDX_HEADS ** -0.5 * IDX_DIM ** -0.5),
        grid=grid,
        in_specs=[pl.BlockSpec((tm, d), row),
                  pl.BlockSpec((d, n_total), lambda i: (0, 0)),
                  pl.BlockSpec((tm, LANES), tab),
                  pl.BlockSpec((tm, LANES), tab),
                  pl.BlockSpec((tm, LANES), tab)],
        out_specs=out_specs,
        out_shape=out_shape,
        compiler_params=_cparams(("parallel",)),
        name="in_proj",
    )(x2d, w_packed, *tabs)


def _block_diag(w, per):
    n, d, _ = w.shape
    g = n // per
    w4 = w.reshape(g, per, d, d)
    eye = jnp.eye(per, dtype=w.dtype)
    out = jnp.einsum("gpde,pq->gpdqe", w4, eye)
    return out.reshape(g, per * d, per * d)


def _gelu_tanh(x):
    return 0.5 * x * (1.0 + jnp.tanh(math.sqrt(2.0 / math.pi) * (x + 0.044715 * (x * x * x))))


def _lru_gates(xc, wa_ref, ba_ref, wi_ref, bi_ref, clam_ref):
    xb = xc.astype(BF16)
    n_tiles = wa_ref.shape[0]
    tw = wa_ref.shape[1]
    ra, ri = [], []
    for g in range(n_tiles):
        xs = xb[:, g * tw:(g + 1) * tw]
        ra.append(_dot(xs, wa_ref[g]))
        ri.append(_dot(xs, wi_ref[g]))
    r = jax.nn.sigmoid(jnp.concatenate(ra, axis=1) + ba_ref[...])
    ig = jax.nn.sigmoid(jnp.concatenate(ri, axis=1) + bi_ref[...])
    log_a = clam_ref[...] * r
    a = jnp.exp(log_a)
    u = jnp.sqrt(-jnp.tanh(log_a) * (a * a + 1.0)) * (ig * xc)
    return a, u


def _scan8(a3, u3):
    t_idx = lax.broadcasted_iota(I32, a3.shape, 1)
    for d in (1, 2, 4):
        keep = t_idx >= d
        a_sh = jnp.where(keep, pltpu.roll(a3, d, axis=1), 1.0)
        u_sh = jnp.where(keep, pltpu.roll(u3, d, axis=1), 0.0)
        u3 = a3 * u_sh + u3
        a3 = a3 * a_sh
    return a3, u3


def _rglru_prompt_kernel(xr_ref, gr_ref, cw_ref, cb_ref, wa_ref, ba_ref, wi_ref, bi_ref, clam_ref,
                         y_ref, hl_ref, xbuf, hc):
    i = pl.program_id(1)
    tt, d = xr_ref.shape

    @pl.when(i == 0)
    def _():
        xbuf[0:SUBLANES, :] = jnp.zeros((SUBLANES, d), F32)
        hc[...] = jnp.zeros_like(hc)

    x = xr_ref[...]
    xbuf[SUBLANES:SUBLANES + tt, :] = x
    xc = cb_ref[...] + cw_ref[CONV_W - 1:CONV_W, :] * x
    for j in range(CONV_W - 1):
        back = CONV_W - 1 - j
        xc = xc + cw_ref[j:j + 1, :] * xbuf[SUBLANES - back:SUBLANES - back + tt, :]
    xbuf[0:SUBLANES, :] = x[tt - SUBLANES:tt, :]

    a, u = _lru_gates(xc, wa_ref, ba_ref, wi_ref, bi_ref, clam_ref)
    g = tt // SUBLANES
    a3, u3 = _scan8(a.reshape(g, SUBLANES, d), u.reshape(g, SUBLANES, d))
    h_in = hc[...]
    hs = []
    for gi in range(g):
        h8 = a3[gi] * h_in + u3[gi]
        hs.append(h8)
        h_in = jnp.broadcast_to(h8[SUBLANES - 1:SUBLANES, :], (SUBLANES, d))
    hc[...] = h_in
    h = jnp.concatenate(hs, axis=0)
    y_ref[...] = (h * _gelu_tanh(gr_ref[...])).astype(y_ref.dtype)

    @pl.when(i == pl.num_programs(1) - 1)
    def _():
        hl_ref[0] = h_in


def _rglru_prompt(xr, gr, b, t, lru_w, tt):
    d = xr.shape[1]
    nt = t // tt
    row = lambda bi, i: (bi * nt + i, 0)
    full2 = lambda bi, i: (0, 0)
    full3 = lambda bi, i: (0, 0, 0)
    cw, cb, wa, ba, wi, bi_, clam = lru_w
    y, hl = pl.pallas_call(
        _rglru_prompt_kernel,
        grid=(b, nt),
        in_specs=[pl.BlockSpec((tt, d), row), pl.BlockSpec((tt, d), row),
                  pl.BlockSpec(cw.shape, full2), pl.BlockSpec(cb.shape, full2),
                  pl.BlockSpec(wa.shape, full3), pl.BlockSpec(ba.shape, full2),
                  pl.BlockSpec(wi.shape, full3), pl.BlockSpec(bi_.shape, full2),
                  pl.BlockSpec(clam.shape, full2)],
        out_specs=(pl.BlockSpec((tt, d), row),
                   pl.BlockSpec((1, SUBLANES, d), lambda bi, i: (bi, 0, 0))),
        out_shape=(jax.ShapeDtypeStruct((b * t, d), BF16),
                   jax.ShapeDtypeStruct((b, SUBLANES, d), F32)),
        scratch_shapes=[pltpu.VMEM((tt + SUBLANES, d), F32), pltpu.VMEM((SUBLANES, d), F32)],
        compiler_params=_cparams(("parallel", "arbitrary")),
        name="rglru_prompt",
    )(xr, gr, cw, cb, wa, ba, wi, bi_, clam)
    return y, hl[:, 0, :]


def _rglru_sample_kernel(xp_ref, gr_ref, h0_ref, cw_ref, cb_ref, wa_ref, ba_ref, wi_ref, bi_ref,
                         clam_ref, y_ref, hs_ref):
    bt, t, d = gr_ref.shape
    xc = cb_ref[...].reshape(1, 1, d)
    for j in range(CONV_W):
        xc = xc + cw_ref[j:j + 1, :].reshape(1, 1, d) * xp_ref[:, j:j + t, :]
    xc2 = xc.reshape(bt * t, d)
    a, u = _lru_gates(xc2, wa_ref, ba_ref, wi_ref, bi_ref, clam_ref)
    a3, u3 = _scan8(a.reshape(bt, t, d), u.reshape(bt, t, d))
    h = a3 * h0_ref[...] + u3
    hs_ref[...] = h
    y_ref[...] = (h * _gelu_tanh(gr_ref[...])).astype(y_ref.dtype)


def _rglru_sample(xp, gr3, h0, lru_w, bt):
    nb, t, d = gr3.shape
    cw, cb, wa, ba, wi, bi_, clam = lru_w
    full2 = lambda i: (0, 0)
    full3 = lambda i: (0, 0, 0)
    blk = lambda i: (i, 0, 0)
    return pl.pallas_call(
        _rglru_sample_kernel,
        grid=(nb // bt,),
        in_specs=[pl.BlockSpec((bt, t + CONV_W - 1, d), blk), pl.BlockSpec((bt, t, d), blk),
                  pl.BlockSpec((bt, 1, d), blk),
                  pl.BlockSpec(cw.shape, full2), pl.BlockSpec(cb.shape, full2),
                  pl.BlockSpec(wa.shape, full3), pl.BlockSpec(ba.shape, full2),
                  pl.BlockSpec(wi.shape, full3), pl.BlockSpec(bi_.shape, full2),
                  pl.BlockSpec(clam.shape, full2)],
        out_specs=(pl.BlockSpec((bt, t, d), blk), pl.BlockSpec((bt, t, d), blk)),
        out_shape=(jax.ShapeDtypeStruct((nb, t, d), BF16), jax.ShapeDtypeStruct((nb, t, d), F32)),
        compiler_params=_cparams(("parallel",)),
        name="rglru_sample",
    )(xp, gr3, h0, cw, cb, wa, ba, wi, bi_, clam)


def _prep_lru(conv_w, conv_b, w_a, b_a, w_i, b_i, lru_lambda):
    d = conv_w.shape[1]
    per = 256 // w_a.shape[1]
    clam = (-LRU_C * jax.nn.softplus(-lru_lambda.astype(F32))).reshape(1, d)
    return (conv_w, conv_b.reshape(1, d), _block_diag(w_a, per).astype(BF16), b_a.reshape(1, d),
            _block_diag(w_i, per).astype(BF16), b_i.reshape(1, d), clam)


INT_MIN = -2 ** 31
INT_MAX = 2 ** 31 - 1
M_INIT = -1e30
L_FLOOR = 1e-30


def _expand_pairs(x, ones_lane=False):
    rows, w = x.shape
    lane = lax.broadcasted_iota(I32, (rows, LANES), 1)
    lo = lane < HEAD_DIM
    z_lo = jnp.where(lane == 0, 1.0, 0.0) if ones_lane else jnp.zeros((rows, LANES), F32)
    z_hi = jnp.where(lane == HEAD_DIM, 1.0, 0.0) if ones_lane else jnp.zeros((rows, LANES), F32)
    outs = []
    for j in range(w // LANES):
        blk = x[:, j * LANES:(j + 1) * LANES]
        rol = pltpu.roll(blk, HEAD_DIM, axis=1)
        outs += [jnp.where(lo, blk, z_hi), jnp.where(lo, z_lo, rol),
                 jnp.where(lo, rol, z_hi), jnp.where(lo, z_lo, blk)]
    return jnp.concatenate(outs, axis=1).astype(BF16)


F32_MANT_BITS = 23
F32_EXP_BITS = 8
MANT_MASK = (1 << F32_MANT_BITS) - 1
EXP_BIAS = 127


def _key_parts(key):
    neg = key < 0
    mag = jnp.where(neg, key ^ INT_MAX, key) & INT_MAX
    return neg, mag >> F32_MANT_BITS, mag & MANT_MASK


def _pow2_biased(e):
    top = jnp.where(e > EXP_BIAS, float(2.0 ** (1 << (F32_EXP_BITS - 2))), 1.0)
    p = top
    for b in range(F32_EXP_BITS - 2, -1, -1):
        p = p * jnp.where(((e >> b) & 1) == 1, 1.0, float(2.0 ** -(1 << b)))
    return p * top


def _mantissa_value(m):
    return (m + (1 << F32_MANT_BITS)).astype(F32) * 2.0 ** -F32_MANT_BITS


def _key_scale(key):
    neg, e, _ = _key_parts(key)
    mag = jnp.where(e == 0, 0.0, _pow2_biased(e))
    return jnp.where(neg, -mag, mag)


def _key_value(key):
    _, _, m = _key_parts(key)
    return _key_scale(key) * _mantissa_value(m)


def _lane_fold(w):
    part = w[:, 0:LANES]
    for j in range(1, w.shape[1] // LANES):
        part = part + w[:, j * LANES:(j + 1) * LANES]
    return part


def _sparse_attend(q, iq, iw, qpos, kexp, vexp, ikexp, nck, tk, n_keep, idx_bits, kmax2, o_ref,
                   key_ref, bias_ref, m_ref, l_ref, acc_ref, j_ref):
    tq = q.shape[0]
    kf = float(n_keep)
    lane_pos = lax.broadcasted_iota(I32, (tq, tk), 1)

    lhs_i = jnp.concatenate([iq[:, m * LANES:(m + 1) * LANES] for m in range(IDX_HEADS // 2)], axis=0)

    def p1(c, carry):
        start = pl.multiple_of(c * tk, tk)
        s_even = _dot_nt(lhs_i, ikexp[0, pl.ds(start, tk), :])
        s_odd = _dot_nt(lhs_i, ikexp[1, pl.ds(start, tk), :])
        sc = jnp.zeros((tq, tk), F32)
        for m in range(IDX_HEADS // 2):
            sc = sc + jnp.maximum(s_even[m * tq:(m + 1) * tq], 0.0) * iw[:, 2 * m:2 * m + 1]
            sc = sc + jnp.maximum(s_odd[m * tq:(m + 1) * tq], 0.0) * iw[:, 2 * m + 1:2 * m + 2]
        key_ref[c] = jnp.where(lane_pos + start <= qpos, sc, NEG_INF)
        return carry

    lax.fori_loop(0, nck, p1, 0)

    def count(weight):
        def body(c, acc):
            return acc + _lane_fold(weight(key_ref[c], c))
        acc = lax.fori_loop(0, nck, body, jnp.zeros((tq, LANES), F32))
        return jnp.sum(acc, axis=1, keepdims=True)

    def count_ge(value):
        return count(lambda kk, c: jnp.where(kk >= value, 1.0, 0.0))

    thr_key = jnp.where(count_ge(0.0) >= kf, jnp.zeros((tq, 1), I32), jnp.full((tq, 1), INT_MIN, I32))

    def exponent_bit(it, key):
        cand = key + jnp.left_shift(jnp.int32(1), 30 - it)
        return jnp.where(count_ge(_key_value(cand)) >= kf, cand, key)

    thr_key = lax.fori_loop(0, F32_EXP_BITS, exponent_bit, thr_key)
    scale = _key_scale(thr_key)
    flip = jnp.where(thr_key < 0, MANT_MASK, 0)

    def mantissa_bit(it, key):
        cand = key + jnp.left_shift(jnp.int32(1), F32_MANT_BITS - 1 - it)
        value = scale * _mantissa_value((cand ^ flip) & MANT_MASK)
        return jnp.where(count_ge(value) >= kf, cand, key)

    thr_key = lax.fori_loop(0, F32_MANT_BITS, mantissa_bit, thr_key)
    thr = _key_value(thr_key)

    cnt_ge = count_ge(thr)
    cnt_gt = count(lambda kk, c: jnp.where(kk > thr, 1.0, 0.0))
    need = kf - cnt_gt
    j_ref[...] = jnp.full((tq, 1), INT_MAX, I32)

    @pl.when(jnp.max(cnt_ge) > kf)
    def _():
        def p3(it, jj):
            cand = jj + jnp.left_shift(jnp.int32(1), idx_bits - 1 - it)
            cnt = count(lambda kk, c: jnp.where(
                kk == thr, jnp.where(lane_pos + c * tk < cand, 1.0, 0.0), 0.0))
            return jnp.where(cnt < need, cand, jj)
        jj = lax.fori_loop(0, idx_bits, p3, jnp.zeros((tq, 1), I32))
        j_ref[...] = jnp.where(cnt_ge > kf, jj, INT_MAX)

    jlast = j_ref[...]

    qscale = HEAD_DIM ** -0.5 * math.log2(math.e)
    lhs = []
    for g in range(N_KV_HEADS):
        blk = jnp.concatenate([q[:, (2 * g) * LANES:(2 * g + 1) * LANES],
                               q[:, (2 * g + 1) * LANES:(2 * g + 2) * LANES]], axis=0)
        lhs.append((blk.astype(F32) * qscale).astype(BF16))
    lo_half = lax.broadcasted_iota(I32, (2 * tq, LANES), 1) < HEAD_DIM

    qmax2 = jnp.zeros((tq, 1), F32)
    for g in range(N_KV_HEADS):
        qf = lhs[g].astype(F32)
        sq = qf * qf
        n2 = jnp.maximum(jnp.sum(jnp.where(lo_half, sq, 0.0), axis=1, keepdims=True),
                         jnp.sum(jnp.where(lo_half, 0.0, sq), axis=1, keepdims=True))
        qmax2 = jnp.maximum(qmax2, jnp.maximum(n2[0:tq], n2[tq:2 * tq]))
    neg_shift = -jnp.sqrt(qmax2 * kmax2)

    def p3b(c, carry):
        kk = key_ref[c]
        kpos = lane_pos + c * tk
        tie = jnp.where(kk == thr, jnp.where(kpos <= jlast, neg_shift, NEG_INF), NEG_INF)
        bias_ref[c] = jnp.where(kpos <= qpos, jnp.where(kk > thr, neg_shift, tie), NEG_INF)
        return carry

    lax.fori_loop(0, nck, p3b, 0)

    def stacks(c):
        start = pl.multiple_of(c * tk, tk)
        b1 = bias_ref[c]
        bias2 = jnp.concatenate([b1, b1], axis=0)
        for g in range(N_KV_HEADS):
            for par in range(2):
                idx = 2 * g + par
                ke = kexp[idx, pl.ds(start, tk), :]
                ve = vexp[idx, pl.ds(start, tk), :]
                yield idx, _dot_nt(lhs[g], ke) + bias2, ve

    def assemble(denoms):
        outs = []
        for g in range(N_KV_HEADS):
            l_lo, l_hi = denoms(g)
            o = (jnp.where(lo_half, acc_ref[2 * g], 0.0) / l_lo
                 + jnp.where(lo_half, 0.0, acc_ref[2 * g + 1]) / l_hi)
            outs.append(o[0:tq])
            outs.append(o[tq:2 * tq])
        o_ref[...] = jnp.concatenate(outs, axis=1).astype(o_ref.dtype)

    def attend_exact():
        m_ref[...] = jnp.full(m_ref.shape, M_INIT, F32)
        l_ref[...] = jnp.zeros(l_ref.shape, F32)
        acc_ref[...] = jnp.zeros(acc_ref.shape, F32)

        def body(c, carry):
            for idx, s, ve in stacks(c):
                m_old = m_ref[idx]
                m_new = jnp.maximum(m_old, jnp.max(s, axis=1, keepdims=True))
                alpha = jnp.exp2(m_old - m_new)
                p = jnp.exp2(s - m_new)
                l_ref[idx] = alpha * l_ref[idx] + jnp.sum(p, axis=1, keepdims=True)
                acc_ref[idx] = alpha * acc_ref[idx] + _dot(p.astype(BF16), ve)
                m_ref[idx] = m_new
            return carry

        lax.fori_loop(0, nck, body, 0)
        assemble(lambda g: (l_ref[2 * g], l_ref[2 * g + 1]))

    acc_ref[...] = jnp.zeros(acc_ref.shape, F32)

    def body(c, carry):
        for idx, s, ve in stacks(c):
            acc_ref[idx] = acc_ref[idx] + _dot(jnp.exp2(s).astype(BF16), ve)
        return carry

    lax.fori_loop(0, nck, body, 0)

    def denoms(g):
        return acc_ref[2 * g][:, HEAD_DIM:HEAD_DIM + 1], acc_ref[2 * g + 1][:, 0:1]

    lmin = jnp.full((2 * tq, 1), jnp.inf, F32)
    for g in range(N_KV_HEADS):
        l_lo, l_hi = denoms(g)
        lmin = jnp.minimum(lmin, jnp.minimum(l_lo, l_hi))
    healthy = jnp.min(lmin) >= L_FLOOR
    assemble(denoms)

    @pl.when(jnp.logical_not(healthy))
    def _():
        attend_exact()


def _attn_scratch(tq, tk, nck_max):
    n_stack = 2 * N_KV_HEADS
    return [pltpu.VMEM((nck_max, tq, tk), F32), pltpu.VMEM((nck_max, tq, tk), F32),
            pltpu.VMEM((n_stack, 2 * tq, 1), F32), pltpu.VMEM((n_stack, 2 * tq, 1), F32),
            pltpu.VMEM((n_stack, 2 * tq, LANES), F32), pltpu.VMEM((tq, 1), I32)]


def _attn_prompt_kernel(q_ref, iq_ref, iw_ref, k_ref, v_ref, ik_ref, o_ref,
                        kexp, vexp, ikexp, kmax_ref, key_ref, bias_ref, m_ref, l_ref, acc_ref, j_ref,
                        *, tk, n_keep, idx_bits):
    i = pl.program_id(1)
    tq = q_ref.shape[0]
    t = k_ref.shape[0]

    @pl.when(i == 0)
    def _():
        kmax_ref[0] = jnp.float32(0.0)

        def fill(c, carry):
            rows = pl.ds(pl.multiple_of(c * tk, tk), tk)
            ke = _expand_pairs(k_ref[rows, :])
            ve = _expand_pairs(v_ref[rows, :], ones_lane=True)
            for s in range(2 * N_KV_HEADS):
                kexp[s, rows, :] = ke[:, s * LANES:(s + 1) * LANES]
                vexp[s, rows, :] = ve[:, s * LANES:(s + 1) * LANES]
            n2 = jnp.zeros((tk, 1), F32)
            for g in range(N_KV_HEADS):
                kf = ke[:, 2 * g * LANES:(2 * g + 1) * LANES].astype(F32)
                n2 = jnp.maximum(n2, jnp.sum(kf * kf, axis=1, keepdims=True))
            kmax_ref[0] = jnp.maximum(kmax_ref[0], jnp.max(n2))
            ik = ik_ref[rows, :].astype(F32)
            ikexp[0, rows, :] = ik.astype(BF16)
            ikexp[1, rows, :] = pltpu.roll(ik, HEAD_DIM, axis=1).astype(BF16)
            return carry
        lax.fori_loop(0, t // tk, fill, 0)

    qpos = i * tq + lax.broadcasted_iota(I32, (tq, 1), 0)
    nck = lax.div((i + 1) * tq - 1, tk) + 1
    _sparse_attend(q_ref[...], iq_ref[...], iw_ref[...], qpos, kexp, vexp, ikexp, nck, tk,
                   n_keep, idx_bits, kmax_ref[0], o_ref, key_ref, bias_ref, m_ref, l_ref, acc_ref, j_ref)


def _attn_prompt(q, iq, iw, k, v, ikb, b, t, tq, tk):
    nq = t // tq
    n_keep = min(TOPK_KEYS, t // 4)
    row = lambda bi, i: (bi * nq + i, 0)
    per_b = lambda bi, i: (bi, 0)
    dq, dk = q.shape[1], k.shape[1]
    kern = functools.partial(_attn_prompt_kernel, tk=tk, n_keep=n_keep,
                             idx_bits=max(1, math.ceil(math.log2(t))))
    return pl.pallas_call(
        kern,
        grid=(b, nq),
        in_specs=[pl.BlockSpec((tq, dq), row), pl.BlockSpec((tq, iq.shape[1]), row),
                  pl.BlockSpec((tq, iw.shape[1]), row),
                  pl.BlockSpec((t, dk), per_b, pipeline_mode=pl.Buffered(1)),
                  pl.BlockSpec((t, dk), per_b, pipeline_mode=pl.Buffered(1)),
                  pl.BlockSpec((t, LANES), per_b, pipeline_mode=pl.Buffered(1))],
        out_specs=pl.BlockSpec((tq, dq), row),
        out_shape=jax.ShapeDtypeStruct((b * t, dq), BF16),
        scratch_shapes=[pltpu.VMEM((2 * N_KV_HEADS, t, LANES), BF16), pltpu.VMEM((2 * N_KV_HEADS, t, LANES), BF16),
                        pltpu.VMEM((2, t, LANES), BF16), pltpu.SMEM((1,), F32)]
        + _attn_scratch(tq, tk, t // tk),
        compiler_params=_cparams(("parallel", "arbitrary")),
        name="attn_prompt",
    )(q, iq, iw, k, v, ikb)


SAMPLE_Q_ROWS = 16


def _entry_rows(src_ref, f32_scr, t_new):
    f32_scr[...] = src_ref[...].astype(F32)
    per_blk = SAMPLE_Q_ROWS // t_new
    off = pl.multiple_of(lax.rem(pl.program_id(0), per_blk) * t_new, t_new)
    return f32_scr[pl.ds(off, t_new), :]


def _sample_geometry(n_pages, page, t_new):
    assert N_HEADS * t_new == LANES and IDX_HEADS * t_new == IDX_DIM and t_new <= page
    past = n_pages * page
    return past, past + page, LANES // t_new


def _with_zero_tail(x, rows):
    return jnp.concatenate([x, jnp.zeros((rows - x.shape[0], x.shape[1]), x.dtype)], axis=0)


def _sample_scores_kernel(pt_ref, iq_ref, w_ref, ikn_ref, *rest, n_pages, t_new, group):
    ikpages = rest[:n_pages]
    o_ref, ikall, iq_scr = rest[n_pages:]
    bl = lax.rem(pl.program_id(0), group)
    page = ikpages[0].shape[1]
    past = n_pages * page
    for p in range(n_pages):
        ikall[p * page:(p + 1) * page, :] = ikpages[p][0].astype(BF16)
    ikall[past:past + page, :] = _with_zero_tail(ikn_ref[...], page).astype(BF16)

    iqf = _entry_rows(iq_ref, iq_scr, t_new)
    pieces = []
    for h in range(IDX_HEADS):
        blk = iqf[:, (h // 2) * LANES:(h // 2 + 1) * LANES]
        if h % 2:
            blk = pltpu.roll(blk, IDX_DIM, axis=1)
        pieces.append(blk[:, :IDX_DIM])
    iqm = jnp.concatenate(pieces, axis=0).astype(BF16)
    r = jnp.maximum(_dot_nt(ikall[...], iqm), 0.0) * w_ref[0]
    hi = r.astype(BF16)
    lo = (r - hi.astype(F32)).astype(BF16)
    r_i = lax.broadcasted_iota(I32, (IDX_DIM, LANES), 0)
    c_i = lax.broadcasted_iota(I32, (IDX_DIM, LANES), 1)
    place = jnp.where(c_i == bl * t_new + lax.rem(r_i, t_new), 1.0, 0.0).astype(BF16)
    placed = _dot(hi, place) + _dot(lo, place)

    @pl.when(bl == 0)
    def _():
        o_ref[0] = placed

    @pl.when(bl != 0)
    def _():
        o_ref[0] = o_ref[0] + placed


def _sample_select_kernel(sc_ref, sel_ref, key_ref, j_ref, *, past, t_new, n_keep, idx_bits):
    s_pad = sc_ref.shape[1]
    row = lax.broadcasted_iota(I32, (s_pad, LANES), 0)
    lane = lax.broadcasted_iota(I32, (s_pad, LANES), 1)
    valid = row <= past + lax.rem(lane, t_new)
    key_ref[...] = jnp.where(valid, sc_ref[0], NEG_INF)
    kf = float(n_keep)

    def count(weight):
        return jnp.sum(weight(key_ref[...]), axis=0, keepdims=True)

    def count_ge(value):
        return count(lambda kk: jnp.where(kk >= value, 1.0, 0.0))

    thr_key = jnp.where(count_ge(0.0) >= kf, jnp.zeros((1, LANES), I32), jnp.full((1, LANES), INT_MIN, I32))

    def key_bit(it, key):
        cand = key + jnp.left_shift(jnp.int32(1), 30 - it)
        return jnp.where(count_ge(_key_value(cand)) >= kf, cand, key)

    thr = _key_value(lax.fori_loop(0, F32_EXP_BITS + F32_MANT_BITS, key_bit, thr_key))
    cnt_ge = count_ge(thr)
    need = kf - count(lambda kk: jnp.where(kk > thr, 1.0, 0.0))
    j_ref[...] = jnp.full(j_ref.shape, INT_MAX, I32)

    @pl.when(jnp.max(cnt_ge) > kf)
    def _():
        def p3(it, jj):
            cand = jj + jnp.left_shift(jnp.int32(1), idx_bits - 1 - it)
            cnt = count(lambda kk: jnp.where(kk == thr, jnp.where(row < cand, 1.0, 0.0), 0.0))
            return jnp.where(cnt < need, cand, jj)
        jj = lax.fori_loop(0, idx_bits, p3, jnp.zeros((1, LANES), I32))
        j_ref[0:1, :] = jnp.where(cnt_ge > kf, jj, INT_MAX)

    jlast = j_ref[0:1, :]
    kk = key_ref[...]
    tie = jnp.where(kk == thr, jnp.where(row <= jlast, 1.0, 0.0), 0.0)
    sel_ref[0] = jnp.where(valid, jnp.where(kk > thr, 1.0, tie), 0.0).astype(sel_ref.dtype)


def _sample_attend_kernel(pt_ref, q_ref, kn_ref, vn_ref, sel_ref, *rest, n_pages, t_new, group):
    kpages = rest[:n_pages]
    vpages = rest[n_pages:2 * n_pages]
    o_ref, kall, vall, q_scr = rest[2 * n_pages:]
    bl = lax.rem(pl.program_id(0), group)
    page = kpages[0].shape[1]
    past = n_pages * page
    s_pad, dk = kall.shape
    for p in range(n_pages):
        rows = slice(p * page, (p + 1) * page)
        kall[rows, :] = kpages[p][0].astype(BF16)
        vall[rows, 0:dk] = vpages[p][0].astype(BF16)
    kall[past:s_pad, :] = _with_zero_tail(kn_ref[...], page).astype(BF16)
    vall[past:s_pad, 0:dk] = _with_zero_tail(vn_ref[...], page).astype(BF16)
    vall[:, dk:dk + LANES] = jnp.where(lax.broadcasted_iota(I32, (s_pad, LANES), 1) == 0, 1.0, 0.0).astype(BF16)

    qf = _entry_rows(q_ref, q_scr, t_new) * (HEAD_DIM ** -0.5 * math.log2(math.e))
    lo = lax.broadcasted_iota(I32, (t_new, LANES), 1) < HEAD_DIM
    zero = jnp.zeros((t_new, LANES), F32)
    per_group = N_HEADS // N_KV_HEADS
    qrows = []
    for h in range(N_HEADS):
        g = h // per_group
        blk = qf[:, (h // 2) * LANES:(h // 2 + 1) * LANES]
        if h % 2 != g % 2:
            blk = pltpu.roll(blk, HEAD_DIM, axis=1)
        blk = jnp.where(lo, blk, zero) if g % 2 == 0 else jnp.where(lo, zero, blk)
        qrows.append(jnp.concatenate([blk, zero] if g // 2 == 0 else [zero, blk], axis=1))
    qbd = jnp.concatenate(qrows, axis=0).astype(BF16)
    s_t = _dot_nt(kall[...], qbd)

    r_i = lax.broadcasted_iota(I32, (LANES, LANES), 0)
    c_i = lax.broadcasted_iota(I32, (LANES, LANES), 1)
    widen = jnp.where(r_i == bl * t_new + lax.rem(c_i, t_new), 1.0, 0.0).astype(BF16)
    s_t = jnp.where(_dot(sel_ref[0], widen) > 0.5, s_t, NEG_INF)
    p = jnp.exp2(s_t - jnp.max(s_t, axis=0, keepdims=True)).astype(BF16)
    o_all = lax.dot_general(p, vall[...], (((0,), (0,)), ((), ())), preferred_element_type=F32)
    o = o_all[:, 0:dk] / o_all[:, dk:dk + 1]

    outs = []
    for m in range(N_HEADS // 2):
        pair = []
        for h in (2 * m, 2 * m + 1):
            g = h // per_group
            piece = o[h * t_new:(h + 1) * t_new, (g // 2) * LANES:(g // 2 + 1) * LANES]
            pair.append(pltpu.roll(piece, HEAD_DIM, axis=1) if g % 2 != h % 2 else piece)
        outs.append(jnp.where(lo, pair[0], pair[1]))
    o_ref[...] = jnp.concatenate(outs, axis=1)


def _attn_sample(q, iq, iw_t, k_new, v_new, ik_new, cache_k, cache_v, cache_ik, page_table, t_new):
    nb, n_pages = page_table.shape
    n_pool, page, dk = cache_k.shape
    past, s_pad, group = _sample_geometry(n_pages, page, t_new)
    per_blk = SAMPLE_Q_ROWS // t_new
    assert nb % group == 0 and nb % per_blk == 0 and t_new % SUBLANES == 0
    n_groups = nb // group
    n_keep = min(TOPK_KEYS, (past + t_new) // 4)
    dq = q.shape[1]
    row = lambda b, pt: (b, 0)
    qrow = lambda b, pt: (b // per_blk, 0)
    grp = lambda b, pt: (b // group, 0, 0)

    def page_spec(width, p):
        return pl.BlockSpec((1, page, width), lambda b, pt, p=p: (pt[b, p], 0, 0))

    scores = pl.pallas_call(
        functools.partial(_sample_scores_kernel, n_pages=n_pages, t_new=t_new, group=group),
        grid_spec=pltpu.PrefetchScalarGridSpec(
            num_scalar_prefetch=1,
            grid=(nb,),
            in_specs=[pl.BlockSpec((SAMPLE_Q_ROWS, iq.shape[1]), qrow),
                      pl.BlockSpec((1, 1, iw_t.shape[2]), lambda b, pt: (b, 0, 0)),
                      pl.BlockSpec((t_new, IDX_DIM), row)] + [page_spec(IDX_DIM, p) for p in range(n_pages)],
            out_specs=pl.BlockSpec((1, s_pad, LANES), grp),
            scratch_shapes=[pltpu.VMEM((s_pad, IDX_DIM), BF16), pltpu.VMEM((SAMPLE_Q_ROWS, iq.shape[1]), F32)]),
        out_shape=jax.ShapeDtypeStruct((n_groups, s_pad, LANES), F32),
        compiler_params=_cparams(("arbitrary",)),
        name="sample_scores",
    )(page_table, iq, iw_t, ik_new, *([cache_ik] * n_pages))

    sel = pl.pallas_call(
        functools.partial(_sample_select_kernel, past=past, t_new=t_new, n_keep=n_keep,
                          idx_bits=max(1, math.ceil(math.log2(s_pad)))),
        grid=(n_groups,),
        in_specs=[pl.BlockSpec((1, s_pad, LANES), lambda g: (g, 0, 0))],
        out_specs=pl.BlockSpec((1, s_pad, LANES), lambda g: (g, 0, 0)),
        out_shape=jax.ShapeDtypeStruct((n_groups, s_pad, LANES), BF16),
        scratch_shapes=[pltpu.VMEM((s_pad, LANES), F32), pltpu.VMEM((SUBLANES, LANES), I32)],
        compiler_params=_cparams(("parallel",)),
        name="sample_select",
    )(scores)

    return pl.pallas_call(
        functools.partial(_sample_attend_kernel, n_pages=n_pages, t_new=t_new, group=group),
        grid_spec=pltpu.PrefetchScalarGridSpec(
            num_scalar_prefetch=1,
            grid=(nb,),
            in_specs=[pl.BlockSpec((SAMPLE_Q_ROWS, dq), qrow),
                      pl.BlockSpec((t_new, dk), row), pl.BlockSpec((t_new, dk), row),
                      pl.BlockSpec((1, s_pad, LANES), grp)]
            + [page_spec(dk, p) for p in range(n_pages)] + [page_spec(dk, p) for p in range(n_pages)],
            out_specs=pl.BlockSpec((t_new, dq), row),
            scratch_shapes=[pltpu.VMEM((s_pad, dk), BF16), pltpu.VMEM((s_pad, dk + LANES), BF16),
                            pltpu.VMEM((SAMPLE_Q_ROWS, dq), F32)]),
        out_shape=jax.ShapeDtypeStruct((nb * t_new, dq), F32),
        compiler_params=_cparams(("parallel",)),
        name="sample_attend",
    )(page_table, q, k_new, v_new, sel, *([cache_k] * n_pages), *([cache_v] * n_pages))


def _layer_norm(x, g, b):
    mu = jnp.mean(x, axis=-1, keepdims=True)
    xc = x - mu
    var = jnp.mean(xc * xc, axis=-1, keepdims=True)
    return xc * lax.rsqrt(var + LN_EPS) * g + b


def _merge_kernel(*refs, alpha, first_tiles):
    wa_ref, wb_ref, wo_ref, g_ref, b_ref, h_ref = refs[10:]

    def run(yr_ref, ya_ref, ga_ref, gb_ref, x_ref):
        pa = _dot(yr_ref[...].astype(BF16), wa_ref[...])
        pb = _dot(ya_ref[...].astype(BF16), wb_ref[...])
        m = jax.nn.sigmoid(ga_ref[...]) * pa + jax.nn.sigmoid(gb_ref[...]) * pb
        mix = _dot(m.astype(BF16), wo_ref[...])
        h_ref[...] = _layer_norm(alpha * x_ref[...] + mix, g_ref[...], b_ref[...])

    @pl.when(pl.program_id(0) < first_tiles)
    def _():
        run(*refs[0:5])

    @pl.when(pl.program_id(0) >= first_tiles)
    def _():
        run(*refs[5:10])


def _merge(first, second, wbr_a, wbr_b, w_out, ln_g, ln_b, alpha, tm):
    n1, d = first[4].shape
    n2 = second[4].shape[0]
    t1, t2 = n1 // tm, n2 // tm
    in_first = lambda i: (jnp.minimum(i, t1 - 1), 0)
    in_second = lambda i: (jnp.maximum(i - t1, 0), 0)
    full = lambda i: (0, 0)
    return pl.pallas_call(
        functools.partial(_merge_kernel, alpha=alpha, first_tiles=t1),
        grid=(t1 + t2,),
        in_specs=[pl.BlockSpec((tm, d), in_first)] * 5 + [pl.BlockSpec((tm, d), in_second)] * 5
        + [pl.BlockSpec((d, d), full)] * 3 + [pl.BlockSpec((1, d), full)] * 2,
        out_specs=pl.BlockSpec((tm, d), lambda i: (i, 0)),
        out_shape=jax.ShapeDtypeStruct((n1 + n2, d), F32),
        compiler_params=_cparams(("arbitrary",)),
        name="merge_ln1",
    )(*first, *second, wbr_a, wbr_b, w_out, ln_g, ln_b)


def _router_kernel(h_ref, whi_ref, wlo_ref, bias_ref, e_ref, g_ref, r_ref, cnt_ref, run_ref):
    h = h_ref[...]
    tm = h.shape[0]
    ne = whi_ref.shape[1]
    h_hi = h.astype(BF16)
    h_lo = (h - h_hi.astype(F32)).astype(BF16)
    logits = _dot(h_hi, whi_ref[...]) + (_dot(h_lo, whi_ref[...]) + _dot(h_hi, wlo_ref[...]))
    scores = jax.nn.sigmoid(logits)
    biased = scores + bias_ref[...]
    lane = lax.broadcasted_iota(I32, (tm, ne), 1)
    lane_f = lane.astype(F32)
    per_group = ne // N_EXPERT_GROUPS
    big = float(ne)

    def first_argmax(v):
        m = jnp.max(v, axis=1, keepdims=True)
        idx = jnp.min(jnp.where(v == m, lane_f, big), axis=1, keepdims=True)
        return m, idx

    gscore = []
    for g in range(N_EXPERT_GROUPS):
        in_g = (lane >= g * per_group) & (lane < (g + 1) * per_group)
        mg = jnp.where(in_g, biased, NEG_INF)
        m1, i1 = first_argmax(mg)
        m2 = jnp.max(jnp.where(lane_f == i1, NEG_INF, mg), axis=1, keepdims=True)
        gscore.append(m1 + m2)

    ok_map = jnp.zeros((tm, ne), F32)
    for g in range(N_EXPERT_GROUPS):
        rank = jnp.zeros((tm, 1), F32)
        for o in range(N_EXPERT_GROUPS):
            if o == g:
                continue
            ahead = (gscore[o] > gscore[g]) if o > g else (gscore[o] >= gscore[g])
            rank = rank + jnp.where(ahead, 1.0, 0.0)
        in_g = (lane >= g * per_group) & (lane < (g + 1) * per_group)
        ok_map = jnp.where(in_g, jnp.where(rank < float(TOPK_GROUPS), 1.0, 0.0), ok_map)

    cur = jnp.where(ok_map > 0.5, biased, NEG_INF)
    out_lane = lax.broadcasted_iota(I32, (tm, LANES), 1)
    e_out = jnp.zeros((tm, LANES), F32)
    s_out = jnp.zeros((tm, LANES), F32)
    total = jnp.zeros((tm, 1), F32)
    picked = jnp.zeros((tm, ne), F32)
    hits = []
    for j in range(EXPERT_TOP_K):
        _, idx = first_argmax(cur)
        hit = lane_f == idx
        hits.append(hit)
        sel = jnp.sum(jnp.where(hit, scores, 0.0), axis=1, keepdims=True)
        cur = jnp.where(hit, NEG_INF, cur)
        picked = jnp.where(hit, 1.0, picked)
        e_out = jnp.where(out_lane == j, idx, e_out)
        s_out = jnp.where(out_lane == j, sel, s_out)
        total = total + sel
    e_ref[...] = e_out
    g_ref[...] = ROUTED_SCALE * s_out / total

    @pl.when(pl.program_id(0) == 0)
    def _():
        run_ref[...] = jnp.zeros_like(run_ref)

    pk = picked.astype(BF16)
    r_i = lax.broadcasted_iota(I32, (tm, tm), 0)
    c_i = lax.broadcasted_iota(I32, (tm, tm), 1)
    before = _dot(jnp.where(c_i < r_i, 1.0, 0.0).astype(BF16), pk) + run_ref[0:1, :]
    r_out = jnp.zeros((tm, LANES), F32)
    for j in range(EXPERT_TOP_K):
        rank = jnp.sum(jnp.where(hits[j], before, 0.0), axis=1, keepdims=True)
        r_out = jnp.where(out_lane == j, rank, r_out)
    r_ref[...] = r_out
    run_ref[...] = run_ref[...] + _dot(jnp.ones((SUBLANES, tm), BF16), pk)
    cnt_ref[...] = run_ref[...]


def _router(h, w_hi, w_lo, bias, tm):
    n, d = h.shape
    ne = w_hi.shape[1]
    row = lambda i: (i, 0)
    full = lambda i: (0, 0)
    return pl.pallas_call(
        _router_kernel,
        grid=(n // tm,),
        in_specs=[pl.BlockSpec((tm, d), row), pl.BlockSpec((d, ne), full), pl.BlockSpec((d, ne), full),
                  pl.BlockSpec((1, ne), full)],
        out_specs=(pl.BlockSpec((tm, LANES), row), pl.BlockSpec((tm, LANES), row),
                   pl.BlockSpec((tm, LANES), row), pl.BlockSpec((SUBLANES, ne), full)),
        out_shape=(jax.ShapeDtypeStruct((n, LANES), F32), jax.ShapeDtypeStruct((n, LANES), F32),
                   jax.ShapeDtypeStruct((n, LANES), F32), jax.ShapeDtypeStruct((SUBLANES, ne), F32)),
        scratch_shapes=[pltpu.VMEM((SUBLANES, ne), F32)],
        compiler_params=_cparams(("arbitrary",)),
        name="router",
    )(h, w_hi, w_lo, bias)


def _row_layout(counts, n_rows):
    n_experts = counts.shape[0]
    n_blocks = n_rows // MOE_BLOCK
    padded = (counts + MOE_BLOCK - 1) // MOE_BLOCK * MOE_BLOCK
    pad_end = jnp.cumsum(padded)
    pad_start = pad_end - padded
    block_start = jnp.arange(n_blocks, dtype=I32) * MOE_BLOCK
    block_expert = jnp.minimum(jnp.sum((pad_end[None, :] <= block_start[:, None]).astype(I32), axis=1),
                               n_experts - 1).astype(I32)
    n_active = (pad_end[-1] // MOE_BLOCK).astype(I32).reshape(1)
    ids = jnp.arange(n_experts, dtype=I32)
    live = counts > 0
    later = jnp.flip(lax.cummin(jnp.flip(jnp.where(live, ids, n_experts))))
    nxt = jnp.concatenate([later[1:], jnp.full((1,), n_experts, I32)])
    next_expert = jnp.where(nxt < n_experts, nxt, -1).astype(I32)
    slot_of = ((jnp.cumsum(live.astype(I32)) - 1) & 1).astype(I32)
    return pad_start.astype(I32), pad_end.astype(I32), block_expert, n_active, next_expert, slot_of


def _dest_kernel(e_ref, r_ref, ps_ref, d_ref):
    e = e_ref[...]
    tm = e.shape[0]
    ne = ps_ref.shape[1]
    lane_f = lax.broadcasted_iota(I32, (tm, ne), 1).astype(F32)
    out_lane = lax.broadcasted_iota(I32, (tm, LANES), 1)
    ps = ps_ref[...]
    out = r_ref[...]
    for j in range(EXPERT_TOP_K):
        start = jnp.sum(jnp.where(lane_f == e[:, j:j + 1], ps, 0.0), axis=1, keepdims=True)
        out = jnp.where(out_lane == j, out + start, out)
    d_ref[...] = out.astype(I32)


def _dest(e_f, r_f, pad_start_f, tm):
    n = e_f.shape[0]
    ne = pad_start_f.shape[1]
    row = lambda i: (i, 0)
    return pl.pallas_call(
        _dest_kernel,
        grid=(n // tm,),
        in_specs=[pl.BlockSpec((tm, LANES), row), pl.BlockSpec((tm, LANES), row),
                  pl.BlockSpec((1, ne), lambda i: (0, 0))],
        out_specs=pl.BlockSpec((tm, LANES), row),
        out_shape=jax.ShapeDtypeStruct((n, LANES), I32),
        compiler_params=_cparams(("parallel",)),
        name="moe_dest",
    )(e_f, r_f, pad_start_f)


def _pow2_below(n):
    return [1 << b for b in range(n.bit_length() - 1, -1, -1)]


def _scatter_kernel(cnt_ref, ps_ref, pe_ref, dest_ref, h_ref, xs_hbm, zbuf, sem, zsem, *, top_k, n_rows):
    i = pl.program_id(0)
    tm = h_ref.shape[0]
    n_experts = cnt_ref.shape[0]

    def row_copy(t, j):
        return pltpu.make_async_copy(h_ref.at[pl.ds(t, 1), :],
                                     xs_hbm.at[pl.ds(dest_ref[0, 0, t * top_k + j], 1), :], sem)

    def issue(t, carry):
        for j in range(top_k):
            row_copy(t, j).start(priority=j % 2)
        return carry

    lax.fori_loop(0, tm, issue, 0)

    def zero_copy(start, size):
        return pltpu.make_async_copy(zbuf.at[pl.ds(0, size), :], xs_hbm.at[pl.ds(start, size), :], zsem)

    @pl.when(i == 0)
    def _():
        zbuf[...] = jnp.zeros_like(zbuf)

        def fill(first, end, wait):
            aligned = (first + (SUBLANES - 1)) & ~(SUBLANES - 1)
            for s in range(SUBLANES - 1):
                @pl.when(first + s < aligned)
                def _():
                    cp = zero_copy(first + s, 1)
                    cp.wait() if wait else cp.start()
            n_tiles = lax.div(end - aligned, jnp.int32(SUBLANES))
            for size in _pow2_below(MOE_BLOCK // SUBLANES - 1):
                @pl.when((n_tiles & size) != 0)
                def _():
                    start = aligned + (n_tiles & ~(2 * size - 1)) * SUBLANES
                    cp = zero_copy(pl.multiple_of(start, SUBLANES), size * SUBLANES)
                    cp.wait() if wait else cp.start()

        def per_expert(wait):
            def body(e, carry):
                fill(ps_ref[e] + cnt_ref[e], pe_ref[e], wait)
                return carry
            lax.fori_loop(0, n_experts, body, 0)

        def tail(wait):
            total = pe_ref[n_experts - 1]

            def body(b, carry):
                cp = zero_copy(pl.multiple_of(total + b * MOE_BLOCK, MOE_BLOCK), MOE_BLOCK)
                cp.wait() if wait else cp.start()
                return carry
            lax.fori_loop(0, lax.div(n_rows - total, jnp.int32(MOE_BLOCK)), body, 0)

        per_expert(False)
        tail(False)
        per_expert(True)
        tail(True)

    pltpu.make_async_copy(h_ref, xs_hbm.at[pl.ds(0, tm), :], sem).wait()
    for _ in range(top_k - 1):
        pltpu.make_async_copy(h_ref, xs_hbm.at[pl.ds(0, tm), :], sem).wait()


def _scatter_rows(h, dest, counts, pad_start, pad_end, n_rows, top_k, tm):
    n, d = h.shape
    grid_spec = pltpu.PrefetchScalarGridSpec(
        num_scalar_prefetch=3,
        grid=(n // tm,),
        in_specs=[pl.BlockSpec((1, 1, tm * top_k), lambda i, *_: (i, 0, 0), memory_space=pltpu.SMEM),
                  pl.BlockSpec((tm, d), lambda i, *_: (i, 0))],
        out_specs=pl.BlockSpec(memory_space=pl.ANY),
        scratch_shapes=[pltpu.VMEM((MOE_BLOCK, d), F32), pltpu.SemaphoreType.DMA(()),
                        pltpu.SemaphoreType.DMA(())])
    return pl.pallas_call(
        functools.partial(_scatter_kernel, top_k=top_k, n_rows=n_rows),
        grid_spec=grid_spec,
        out_shape=jax.ShapeDtypeStruct((n_rows, d), F32),
        compiler_params=_cparams(("arbitrary",)),
        name="moe_scatter",
    )(counts, pad_start, pad_end, dest.reshape(n // tm, 1, tm * top_k), h)


def _experts_kernel(be_ref, na_ref, nxt_ref, slot_ref, x_ref, wg_hbm, wu_hbm, wd_hbm, o_ref,
                    wg_f, wu_f, wd_f, wg_b, wu_b, wd_b, sem):
    i = pl.program_id(0)

    def fetch(e, slot):
        return (pltpu.make_async_copy(wg_hbm.at[e], wg_f.at[slot], sem.at[slot, 0]),
                pltpu.make_async_copy(wu_hbm.at[e], wu_f.at[slot], sem.at[slot, 1]),
                pltpu.make_async_copy(wd_hbm.at[e], wd_f.at[slot], sem.at[slot, 2]))

    @pl.when(i < na_ref[0])
    def _():
        e = be_ref[i]
        slot = slot_ref[e]

        @pl.when(i == 0)
        def _():
            for cp in fetch(e, slot):
                cp.start()

        @pl.when((i == 0) | (be_ref[jnp.maximum(i - 1, 0)] != e))
        def _():
            for cp in fetch(e, slot):
                cp.wait()
            nxt = nxt_ref[e]

            @pl.when(nxt >= 0)
            def _():
                for cp in fetch(nxt, 1 - slot):
                    cp.start()

            wg_b[...] = wg_f[slot].astype(BF16)
            wu_b[...] = wu_f[slot].astype(BF16)
            wd_b[...] = wd_f[slot].astype(BF16)

        xb = x_ref[...].astype(BF16)
        g = _dot(xb, wg_b[...])
        u = _dot(xb, wu_b[...])
        hb = (g * jax.nn.sigmoid(g)) * u
        o_ref[...] = _dot(hb.astype(BF16), wd_b[...])

    @pl.when(i >= na_ref[0])
    def _():
        o_ref[...] = jnp.zeros(o_ref.shape, o_ref.dtype)


def _experts(xs, block_expert, n_active, next_expert, slot_of, w_gate, w_up, w_down):
    n_rows, d = xs.shape
    ne, _, de = w_gate.shape
    n_blocks = n_rows // MOE_BLOCK
    live = lambda i, be, na, nx, sl: (jnp.minimum(i, na[0] - 1), 0)
    grid_spec = pltpu.PrefetchScalarGridSpec(
        num_scalar_prefetch=4,
        grid=(n_blocks,),
        in_specs=[pl.BlockSpec((MOE_BLOCK, d), live)] + [pl.BlockSpec(memory_space=pl.ANY)] * 3,
        out_specs=pl.BlockSpec((MOE_BLOCK, d), lambda i, be, na, nx, sl: (i, 0)),
        scratch_shapes=[pltpu.VMEM((2, d, de), F32), pltpu.VMEM((2, d, de), F32), pltpu.VMEM((2, de, d), F32),
                        pltpu.VMEM((d, de), BF16), pltpu.VMEM((d, de), BF16), pltpu.VMEM((de, d), BF16),
                        pltpu.SemaphoreType.DMA((2, 3))])
    return pl.pallas_call(
        _experts_kernel,
        grid_spec=grid_spec,
        out_shape=jax.ShapeDtypeStruct((n_rows, d), F32),
        compiler_params=_cparams(("arbitrary",)),
        name="moe_experts",
    )(block_expert, n_active, next_expert, slot_of, xs, w_gate, w_up, w_down)


def _combine_kernel(dest_ref, rows_hbm, h_ref, gate_ref, wsg_ref, wsu_ref, wsd_ref, g_ref, b_ref, y_ref,
                    gbuf, sem, *, alpha, top_k):
    tm = h_ref.shape[0]

    def issue(t, carry):
        for j in range(top_k):
            pltpu.make_async_copy(rows_hbm.at[pl.ds(dest_ref[0, 0, t * top_k + j], 1), :],
                                  gbuf.at[j, pl.ds(t, 1), :], sem).start(priority=j % 2)
        return carry

    lax.fori_loop(0, tm, issue, 0)
    h = h_ref[...]
    hb = h.astype(BF16)
    sg = _dot(hb, wsg_ref[...])
    su = _dot(hb, wsu_ref[...])
    shared = _dot(((sg * jax.nn.sigmoid(sg)) * su).astype(BF16), wsd_ref[...])
    gates = gate_ref[...]
    for j in range(top_k):
        pltpu.make_async_copy(rows_hbm.at[pl.ds(0, tm), :], gbuf.at[j], sem).wait()
    routed = gbuf[0] * gates[:, 0:1]
    for j in range(1, top_k):
        routed = routed + gbuf[j] * gates[:, j:j + 1]
    y_ref[...] = _layer_norm(alpha * h + (routed + shared), g_ref[...], b_ref[...])


def _combine(h, rows, dest, gates, wsg, wsu, wsd, ln_g, ln_b, alpha, top_k, tm):
    n, d = h.shape
    ds_ = wsg.shape[1]
    row = lambda i: (i, 0)
    full = lambda i: (0, 0)
    return pl.pallas_call(
        functools.partial(_combine_kernel, alpha=alpha, top_k=top_k),
        grid=(n // tm,),
        in_specs=[pl.BlockSpec((1, 1, tm * top_k), lambda i: (i, 0, 0), memory_space=pltpu.SMEM),
                  pl.BlockSpec(memory_space=pl.ANY),
                  pl.BlockSpec((tm, d), row), pl.BlockSpec((tm, LANES), row),
                  pl.BlockSpec((d, ds_), full), pl.BlockSpec((d, ds_), full), pl.BlockSpec((ds_, d), full),
                  pl.BlockSpec((1, d), full), pl.BlockSpec((1, d), full)],
        out_specs=pl.BlockSpec((tm, d), row),
        out_shape=jax.ShapeDtypeStruct((n, d), F32),
        scratch_shapes=[pltpu.VMEM((top_k, tm, d), F32), pltpu.SemaphoreType.DMA(())],
        compiler_params=_cparams(("arbitrary",)),
        name="moe_combine",
    )(dest.reshape(n // tm, 1, tm * top_k), rows, h, gates, wsg, wsu, wsd, ln_g, ln_b)


def _moe(h, w_r_hi, w_r_lo, r_bias, w_gate, w_up, w_down, wsg, wsu, wsd, ln_g, ln_b, alpha):
    n, d = h.shape
    ne = w_r_hi.shape[1]
    top_k = EXPERT_TOP_K
    n_rows = -(-(n * top_k + ne * (MOE_BLOCK - 1)) // MOE_BLOCK) * MOE_BLOCK
    e_f, g_f, r_f, cnt = _router(h, w_r_hi, w_r_lo, r_bias, 256)
    counts = cnt[0].astype(I32)
    pad_start, pad_end, block_expert, n_active, next_expert, slot_of = _row_layout(counts, n_rows)
    dest = _dest(e_f, r_f, pad_start.astype(F32).reshape(1, ne), 256)[:, :top_k]
    xs = _scatter_rows(h, dest, counts, pad_start, pad_end, n_rows, top_k, 128)
    ys = _experts(xs, block_expert, n_active, next_expert, slot_of, w_gate, w_up, w_down)
    return _combine(h, ys, dest, g_f, wsg, wsu, wsd, ln_g, ln_b, alpha, top_k, 128)


def _layer(xp, xs, cache_k, cache_v, cache_idx_k, state_conv, state_h, page_table, depth,
           w_in, conv_w, conv_b, w_a, b_a, w_i, b_i, lru_lambda, w_branch, w_out, ln1_g, ln1_b,
           w_router, router_bias, w_exp_gate, w_exp_up, w_exp_down, w_sh_gate, w_sh_up, w_sh_down,
           ln2_g, ln2_b):
    bp, tp, d = xp.shape
    bs, ts, _ = xs.shape
    n_pages = page_table.shape[1]
    past = n_pages * PAGE_SIZE
    alpha = (2.0 * depth) ** 0.25
    d_rnn = conv_w.shape[1]
    dkv = N_KV_HEADS * HEAD_DIM

    w_packed = _pack_w_in(w_in)
    lru_w = _prep_lru(conv_w, conv_b, w_a, b_a, w_i, b_i, lru_lambda)

    np_ = bp * tp
    tabs_p = _rope_tables(jnp.arange(tp, dtype=I32))
    tm = 256
    (xr, gr, q, k_p, v_p, iq, ik_p, ikb, iw, ga_p, gb_p) = _project(
        xp.reshape(np_, d), w_packed, tabs_p, tp // tm, tm)
    y_rnn_p, h_p = _rglru_prompt(xr, gr, bp, tp, lru_w, 256)
    conv_p = xr.reshape(bp, tp, d_rnn)[:, tp - (CONV_W - 1):, :]
    y_att_p = _attn_prompt(q, iq, iw, k_p, v_p, ikb, bp, tp, 256, 512)

    ns = bs * ts
    pos_s = past + jnp.arange(ts, dtype=I32)
    tabs_s = tuple(jnp.tile(t_, (ns // ts, 1)) for t_ in _rope_tables(pos_s))
    (xr_s, gr_s, q_s, k_s, v_s, iq_s, ik_s, _, iw_s, ga_s, gb_s) = _project(
        xs.reshape(ns, d), w_packed, tabs_s, ns // tm, tm)
    xp_s = jnp.concatenate([state_conv.astype(F32), xr_s.reshape(bs, ts, d_rnn)], axis=1)
    y_rnn_s, hs_s = _rglru_sample(xp_s, gr_s.reshape(bs, ts, d_rnn), state_h.reshape(bs, 1, d_rnn), lru_w, 16)
    conv_s = xp_s[:, ts:, :]
    h_s = hs_s[:, ts - 1, :]

    n_pool = cache_k.shape[0]
    iw_t = iw_s.reshape(bs, ts, IDX_HEADS).transpose(0, 2, 1).reshape(bs, 1, IDX_HEADS * ts)
    y_att_s = _attn_sample(q_s, iq_s, iw_t, k_s, v_s, ik_s,
                           cache_k.reshape(n_pool, PAGE_SIZE, dkv), cache_v.reshape(n_pool, PAGE_SIZE, dkv),
                           cache_idx_k, page_table, ts)

    wbr = w_branch.astype(BF16)
    h1 = _merge((y_rnn_p, y_att_p, ga_p, gb_p, xp.reshape(np_, d)),
                (y_rnn_s.reshape(ns, d_rnn), y_att_s, ga_s, gb_s, xs.reshape(ns, d)),
                wbr[:d_rnn], wbr[d_rnn:], w_out.astype(BF16),
                ln1_g.reshape(1, d), ln1_b.reshape(1, d), alpha, 256)

    w_r = w_router.astype(F32)
    w_r_hi = w_r.astype(BF16)
    w_r_lo = (w_r - w_r_hi.astype(F32)).astype(BF16)
    ne = w_router.shape[1]
    y = _moe(h1, w_r_hi, w_r_lo, router_bias.reshape(1, ne).astype(F32), w_exp_gate, w_exp_up, w_exp_down,
             w_sh_gate.astype(BF16), w_sh_up.astype(BF16), w_sh_down.astype(BF16),
             ln2_g.reshape(1, d), ln2_b.reshape(1, d), alpha)

    yp = y[:np_].reshape(bp, tp, d)
    ys = y[np_:].reshape(bs, ts, d)
    st = (k_p.reshape(bp, tp, N_KV_HEADS, HEAD_DIM), v_p.reshape(bp, tp, N_KV_HEADS, HEAD_DIM),
          ik_p.reshape(bp, tp, IDX_DIM), conv_p, h_p,
          k_s.reshape(bs, ts, N_KV_HEADS, HEAD_DIM), v_s.reshape(bs, ts, N_KV_HEADS, HEAD_DIM),
          ik_s.reshape(bs, ts, IDX_DIM), conv_s, h_s)
    return yp, ys, st


def kernel(x_prompt, x_sample, cache_k, cache_v, cache_idx_k, state_conv, state_h, page_table, w_in, conv_w,
           conv_b, w_a, b_a, w_i, b_i, lru_lambda, w_branch, w_out, ln1_g, ln1_b, w_router, router_bias,
           w_exp_gate, w_exp_up, w_exp_down, w_sh_gate, w_sh_up, w_sh_down, ln2_g, ln2_b):
    depth = w_in.shape[0]
    yp, ys = x_prompt, x_sample
    states = []
    for l in range(depth):
        yp, ys, st = _layer(
            yp, ys, cache_k[l], cache_v[l], cache_idx_k[l], state_conv[l], state_h[l], page_table, depth,
            w_in[l], conv_w[l], conv_b[l], w_a[l], b_a[l], w_i[l], b_i[l], lru_lambda[l], w_branch[l], w_out[l],
            ln1_g[l], ln1_b[l], w_router[l], router_bias[l], w_exp_gate[l], w_exp_up[l], w_exp_down[l],
            w_sh_gate[l], w_sh_up[l], w_sh_down[l], ln2_g[l], ln2_b[l])
        states.append(st)
    stacked = [jnp.stack(c) for c in zip(*states)]
    return (yp, ys, *stacked)
```

```python
import functools
import math

import jax
import jax.numpy as jnp
from jax import lax
from jax.experimental import pallas as pl
from jax.experimental.pallas import tpu as pltpu

F32 = jnp.float32
BF16 = jnp.bfloat16
I32 = jnp.int32

LRU_BLOCKS = 16
CONV_W = 4
LRU_C = 8.0
N_HEADS = 16
N_KV_HEADS = 4
HEAD_DIM = 64
ROT_DIM = 16
ROPE_THETA = 500000.0
IDX_HEADS = 8
IDX_DIM = 64
TOPK_KEYS = 256
PAGE_SIZE = 128
N_EXPERT_GROUPS = 8
TOPK_GROUPS = 4
EXPERT_TOP_K = 8
ROUTED_SCALE = 2.5
MOE_BLOCK = 256
LN_EPS = 1e-5

LANES = 128
SUBLANES = 8
VMEM_LIMIT = 56 * 1024 * 1024

NEG_INF = float("-inf")


def _cparams(sem):
    return pltpu.CompilerParams(dimension_semantics=sem, vmem_limit_bytes=VMEM_LIMIT)


def _dot(a, b):
    return jnp.dot(a, b, preferred_element_type=F32)


def _dot_nt(a, b):
    return lax.dot_general(a, b, (((1,), (1,)), ((), ())), preferred_element_type=F32)


_PROJ_GROUPS = (
    ("xr", 1024, 1024, False),
    ("gr", 1024, 1024, False),
    ("q", 1024, 1024, True),
    ("k", 256, 256, True),
    ("v", 256, 256, False),
    ("iq", 512, 512, True),
    ("ik", 64, 128, True),
    ("iw", 8, 128, False),
    ("ga", 1024, 1024, False),
    ("gb", 1024, 1024, False),
)


def _pack_w_in(w_in):
    cols = []
    c0 = 0
    for _, w, wp, _ in _PROJ_GROUPS:
        blk = w_in[:, c0:c0 + w]
        if wp != w:
            blk = jnp.pad(blk, ((0, 0), (0, wp - w)))
        cols.append(blk)
        c0 += w
    return jnp.concatenate(cols, axis=1).astype(BF16)


def _rope_tables(pos):
    half = ROT_DIM // 2
    inv_freq = jnp.power(ROPE_THETA, -jnp.arange(half, dtype=F32) * (2.0 / ROT_DIM))
    ang = pos.astype(F32)[:, None] * inv_freq[None, :]
    cos = jnp.cos(ang)
    sin = jnp.sin(ang)
    t = pos.shape[0]
    rest = HEAD_DIM - ROT_DIM
    c = jnp.concatenate([cos, cos, jnp.ones((t, rest), F32)], axis=1)
    sa = jnp.concatenate([-sin, jnp.zeros((t, half + rest), F32)], axis=1)
    sb = jnp.concatenate([jnp.zeros((t, half), F32), sin, jnp.zeros((t, rest), F32)], axis=1)
    rep = LANES // HEAD_DIM
    return jnp.tile(c, (1, rep)), jnp.tile(sa, (1, rep)), jnp.tile(sb, (1, rep))


def _proj_kernel(x_ref, w_ref, c_ref, sa_ref, sb_ref,
                 xr_ref, gr_ref, q_ref, k_ref, v_ref, iq_ref,
                 ik_ref, ikb_ref, iw_ref, ga_ref, gb_ref, *, iw_scale):
    xb = x_ref[...].astype(BF16)
    c = c_ref[...]
    sa = sa_ref[...]
    sb = sb_ref[...]

    def rope(z):
        n = z.shape[1] // LANES
        cc = jnp.concatenate([c] * n, axis=1) if n > 1 else c
        aa = jnp.concatenate([sa] * n, axis=1) if n > 1 else sa
        bb = jnp.concatenate([sb] * n, axis=1) if n > 1 else sb
        half = ROT_DIM // 2
        up = pltpu.roll(z, z.shape[1] - half, axis=1)
        dn = pltpu.roll(z, half, axis=1)
        return z * cc + up * aa + dn * bb

    c0 = 0
    zs = {}
    for name, _, wp, rot in _PROJ_GROUPS:
        z = _dot(xb, w_ref[:, c0:c0 + wp])
        zs[name] = rope(z) if rot else z
        c0 += wp
    xr_ref[...] = zs["xr"]
    gr_ref[...] = zs["gr"]
    q_ref[...] = zs["q"].astype(BF16)
    k_ref[...] = zs["k"]
    v_ref[...] = zs["v"]
    iq_ref[...] = zs["iq"].astype(BF16)
    ik_ref[...] = zs["ik"][:, :IDX_DIM]
    ikb_ref[...] = zs["ik"].astype(BF16)
    iw_ref[...] = zs["iw"][:, :IDX_HEADS] * iw_scale
    ga_ref[...] = zs["ga"]
    gb_ref[...] = zs["gb"]


def _project(x2d, w_packed, tabs, n_tab_blocks, tm):
    m, d = x2d.shape
    n_total = w_packed.shape[1]
    grid = (m // tm,)
    row = lambda i: (i, 0)
    tab = lambda i: (i % n_tab_blocks, 0)
    out_defs = (
        (1024, F32), (1024, F32), (1024, BF16), (256, F32), (256, F32),
        (512, BF16), (IDX_DIM, F32), (LANES, BF16), (IDX_HEADS, F32), (1024, F32), (1024, F32))
    out_shape = tuple(jax.ShapeDtypeStruct((m, w), dt) for w, dt in out_defs)
    out_specs = tuple(pl.BlockSpec((tm, w), row) for w, _ in out_defs)
    return pl.pallas_call(
        functools.partial(_proj_kernel, iw_scale=IDX_HEADS ** -0.5 * IDX_DIM ** -0.5),
        grid=grid,
        in_specs=[pl.BlockSpec((tm, d), row),
                  pl.BlockSpec((d, n_total), lambda i: (0, 0)),
                  pl.BlockSpec((tm, LANES), tab),
                  pl.BlockSpec((tm, LANES), tab),
                  pl.BlockSpec((tm, LANES), tab)],
        out_specs=out_specs,
        out_shape=out_shape,
        compiler_params=_cparams(("parallel",)),
        name="in_proj",
    )(x2d, w_packed, *tabs)


def _block_diag(w, per):
    n, d, _ = w.shape
    g = n // per
    w4 = w.reshape(g, per, d, d)
    eye = jnp.eye(per, dtype=w.dtype)
    out = jnp.einsum("gpde,pq->gpdqe", w4, eye)
    return out.reshape(g, per * d, per * d)


def _gelu_tanh(x):
    return 0.5 * x * (1.0 + jnp.tanh(math.sqrt(2.0 / math.pi) * (x + 0.044715 * (x * x * x))))


def _lru_gates(xc, wa_ref, ba_ref, wi_ref, bi_ref, clam_ref):
    xb = xc.astype(BF16)
    n_tiles = wa_ref.shape[0]
    tw = wa_ref.shape[1]
    ra, ri = [], []
    for g in range(n_tiles):
        xs = xb[:, g * tw:(g + 1) * tw]
        ra.append(_dot(xs, wa_ref[g]))
        ri.append(_dot(xs, wi_ref[g]))
    r = jax.nn.sigmoid(jnp.concatenate(ra, axis=1) + ba_ref[...])
    ig = jax.nn.sigmoid(jnp.concatenate(ri, axis=1) + bi_ref[...])
    log_a = clam_ref[...] * r
    a = jnp.exp(log_a)
    u = jnp.sqrt(-jnp.tanh(log_a) * (a * a + 1.0)) * (ig * xc)
    return a, u


def _scan8(a3, u3):
    t_idx = lax.broadcasted_iota(I32, a3.shape, 1)
    for d in (1, 2, 4):
        keep = t_idx >= d
        a_sh = jnp.where(keep, pltpu.roll(a3, d, axis=1), 1.0)
        u_sh = jnp.where(keep, pltpu.roll(u3, d, axis=1), 0.0)
        u3 = a3 * u_sh + u3
        a3 = a3 * a_sh
    return a3, u3


def _rglru_prompt_kernel(xr_ref, gr_ref, cw_ref, cb_ref, wa_ref, ba_ref, wi_ref, bi_ref, clam_ref,
                         y_ref, hl_ref, xbuf, hc):
    i = pl.program_id(1)
    tt, d = xr_ref.shape

    @pl.when(i == 0)
    def _():
        xbuf[0:SUBLANES, :] = jnp.zeros((SUBLANES, d), F32)
        hc[...] = jnp.zeros_like(hc)

    x = xr_ref[...]
    xbuf[SUBLANES:SUBLANES + tt, :] = x
    xc = cb_ref[...] + cw_ref[CONV_W - 1:CONV_W, :] * x
    for j in range(CONV_W - 1):
        back = CONV_W - 1 - j
        xc = xc + cw_ref[j:j + 1, :] * xbuf[SUBLANES - back:SUBLANES - back + tt, :]
    xbuf[0:SUBLANES, :] = x[tt - SUBLANES:tt, :]

    a, u = _lru_gates(xc, wa_ref, ba_ref, wi_ref, bi_ref, clam_ref)
    g = tt // SUBLANES
    a3, u3 = _scan8(a.reshape(g, SUBLANES, d), u.reshape(g, SUBLANES, d))
    h_in = hc[...]
    hs = []
    for gi in range(g):
        h8 = a3[gi] * h_in + u3[gi]
        hs.append(h8)
        h_in = jnp.broadcast_to(h8[SUBLANES - 1:SUBLANES, :], (SUBLANES, d))
    hc[...] = h_in
    h = jnp.concatenate(hs, axis=0)
    y_ref[...] = (h * _gelu_tanh(gr_ref[...])).astype(y_ref.dtype)

    @pl.when(i == pl.num_programs(1) - 1)
    def _():
        hl_ref[0] = h_in


def _rglru_prompt(xr, gr, b, t, lru_w, tt):
    d = xr.shape[1]
    nt = t // tt
    row = lambda bi, i: (bi * nt + i, 0)
    full2 = lambda bi, i: (0, 0)
    full3 = lambda bi, i: (0, 0, 0)
    cw, cb, wa, ba, wi, bi_, clam = lru_w
    y, hl = pl.pallas_call(
        _rglru_prompt_kernel,
        grid=(b, nt),
        in_specs=[pl.BlockSpec((tt, d), row), pl.BlockSpec((tt, d), row),
                  pl.BlockSpec(cw.shape, full2), pl.BlockSpec(cb.shape, full2),
                  pl.BlockSpec(wa.shape, full3), pl.BlockSpec(ba.shape, full2),
                  pl.BlockSpec(wi.shape, full3), pl.BlockSpec(bi_.shape, full2),
                  pl.BlockSpec(clam.shape, full2)],
        out_specs=(pl.BlockSpec((tt, d), row),
                   pl.BlockSpec((1, SUBLANES, d), lambda bi, i: (bi, 0, 0))),
        out_shape=(jax.ShapeDtypeStruct((b * t, d), BF16),
                   jax.ShapeDtypeStruct((b, SUBLANES, d), F32)),
        scratch_shapes=[pltpu.VMEM((tt + SUBLANES, d), F32), pltpu.VMEM((SUBLANES, d), F32)],
        compiler_params=_cparams(("parallel", "arbitrary")),
        name="rglru_prompt",
    )(xr, gr, cw, cb, wa, ba, wi, bi_, clam)
    return y, hl[:, 0, :]


def _rglru_sample_kernel(xp_ref, gr_ref, h0_ref, cw_ref, cb_ref, wa_ref, ba_ref, wi_ref, bi_ref,
                         clam_ref, y_ref, hs_ref):
    bt, t, d = gr_ref.shape
    xc = cb_ref[...].reshape(1, 1, d)
    for j in range(CONV_W):
        xc = xc + cw_ref[j:j + 1, :].reshape(1, 1, d) * xp_ref[:, j:j + t, :]
    xc2 = xc.reshape(bt * t, d)
    a, u = _lru_gates(xc2, wa_ref, ba_ref, wi_ref, bi_ref, clam_ref)
    a3, u3 = _scan8(a.reshape(bt, t, d), u.reshape(bt, t, d))
    h = a3 * h0_ref[...] + u3
    hs_ref[...] = h
    y_ref[...] = (h * _gelu_tanh(gr_ref[...])).astype(y_ref.dtype)


def _rglru_sample(xp, gr3, h0, lru_w, bt):
    nb, t, d = gr3.shape
    cw, cb, wa, ba, wi, bi_, clam = lru_w
    full2 = lambda i: (0, 0)
    full3 = lambda i: (0, 0, 0)
    blk = lambda i: (i, 0, 0)
    return pl.pallas_call(
        _rglru_sample_kernel,
        grid=(nb // bt,),
        in_specs=[pl.BlockSpec((bt, t + CONV_W - 1, d), blk), pl.BlockSpec((bt, t, d), blk),
                  pl.BlockSpec((bt, 1, d), blk),
                  pl.BlockSpec(cw.shape, full2), pl.BlockSpec(cb.shape, full2),
                  pl.BlockSpec(wa.shape, full3), pl.BlockSpec(ba.shape, full2),
                  pl.BlockSpec(wi.shape, full3), pl.BlockSpec(bi_.shape, full2),
                  pl.BlockSpec(clam.shape, full2)],
        out_specs=(pl.BlockSpec((bt, t, d), blk), pl.BlockSpec((bt, t, d), blk)),
        out_shape=(jax.ShapeDtypeStruct((nb, t, d), BF16), jax.ShapeDtypeStruct((nb, t, d), F32)),
        compiler_params=_cparams(("parallel",)),
        name="rglru_sample",
    )(xp, gr3, h0, cw, cb, wa, ba, wi, bi_, clam)


def _prep_lru(conv_w, conv_b, w_a, b_a, w_i, b_i, lru_lambda):
    d = conv_w.shape[1]
    per = 256 // w_a.shape[1]
    clam = (-LRU_C * jax.nn.softplus(-lru_lambda.astype(F32))).reshape(1, d)
    return (conv_w, conv_b.reshape(1, d), _block_diag(w_a, per).astype(BF16), b_a.reshape(1, d),
            _block_diag(w_i, per).astype(BF16), b_i.reshape(1, d), clam)


INT_MIN = -2 ** 31
INT_MAX = 2 ** 31 - 1
M_INIT = -1e30
L_FLOOR = 1e-30


def _expand_pairs(x, ones_lane=False):
    rows, w = x.shape
    lane = lax.broadcasted_iota(I32, (rows, LANES), 1)
    lo = lane < HEAD_DIM
    z_lo = jnp.where(lane == 0, 1.0, 0.0) if ones_lane else jnp.zeros((rows, LANES), F32)
    z_hi = jnp.where(lane == HEAD_DIM, 1.0, 0.0) if ones_lane else jnp.zeros((rows, LANES), F32)
    outs = []
    for j in range(w // LANES):
        blk = x[:, j * LANES:(j + 1) * LANES]
        rol = pltpu.roll(blk, HEAD_DIM, axis=1)
        outs += [jnp.where(lo, blk, z_hi), jnp.where(lo, z_lo, rol),
                 jnp.where(lo, rol, z_hi), jnp.where(lo, z_lo, blk)]
    return jnp.concatenate(outs, axis=1).astype(BF16)


F32_MANT_BITS = 23
F32_EXP_BITS = 8
MANT_MASK = (1 << F32_MANT_BITS) - 1
EXP_BIAS = 127


def _key_parts(key):
    neg = key < 0
    mag = jnp.where(neg, key ^ INT_MAX, key) & INT_MAX
    return neg, mag >> F32_MANT_BITS, mag & MANT_MASK


def _pow2_biased(e):
    top = jnp.where(e > EXP_BIAS, float(2.0 ** (1 << (F32_EXP_BITS - 2))), 1.0)
    p = top
    for b in range(F32_EXP_BITS - 2, -1, -1):
        p = p * jnp.where(((e >> b) & 1) == 1, 1.0, float(2.0 ** -(1 << b)))
    return p * top


def _mantissa_value(m):
    return (m + (1 << F32_MANT_BITS)).astype(F32) * 2.0 ** -F32_MANT_BITS


def _key_scale(key):
    neg, e, _ = _key_parts(key)
    mag = jnp.where(e == 0, 0.0, _pow2_biased(e))
    return jnp.where(neg, -mag, mag)


def _key_value(key):
    _, _, m = _key_parts(key)
    return _key_scale(key) * _mantissa_value(m)


def _lane_fold(w):
    part = w[:, 0:LANES]
    for j in range(1, w.shape[1] // LANES):
        part = part + w[:, j * LANES:(j + 1) * LANES]
    return part


def _sparse_attend(q, iq, iw, qpos, kexp, vexp, ikexp, nck, tk, n_keep, idx_bits, kmax2, o_ref,
                   key_ref, bias_ref, m_ref, l_ref, acc_ref, j_ref):
    tq = q.shape[0]
    kf = float(n_keep)
    lane_pos = lax.broadcasted_iota(I32, (tq, tk), 1)

    lhs_i = jnp.concatenate([iq[:, m * LANES:(m + 1) * LANES] for m in range(IDX_HEADS // 2)], axis=0)

    def p1(c, carry):
        start = pl.multiple_of(c * tk, tk)
        s_even = _dot_nt(lhs_i, ikexp[0, pl.ds(start, tk), :])
        s_odd = _dot_nt(lhs_i, ikexp[1, pl.ds(start, tk), :])
        sc = jnp.zeros((tq, tk), F32)
        for m in range(IDX_HEADS // 2):
            sc = sc + jnp.maximum(s_even[m * tq:(m + 1) * tq], 0.0) * iw[:, 2 * m:2 * m + 1]
            sc = sc + jnp.maximum(s_odd[m * tq:(m + 1) * tq], 0.0) * iw[:, 2 * m + 1:2 * m + 2]
        key_ref[c] = jnp.where(lane_pos + start <= qpos, sc, NEG_INF)
        return carry

    lax.fori_loop(0, nck, p1, 0)

    def count(weight):
        def body(c, acc):
            return acc + _lane_fold(weight(key_ref[c], c))
        acc = lax.fori_loop(0, nck, body, jnp.zeros((tq, LANES), F32))
        return jnp.sum(acc, axis=1, keepdims=True)

    def count_ge(value):
        return count(lambda kk, c: jnp.where(kk >= value, 1.0, 0.0))

    thr_key = jnp.where(count_ge(0.0) >= kf, jnp.zeros((tq, 1), I32), jnp.full((tq, 1), INT_MIN, I32))

    def exponent_bit(it, key):
        cand = key + jnp.left_shift(jnp.int32(1), 30 - it)
        return jnp.where(count_ge(_key_value(cand)) >= kf, cand, key)

    thr_key = lax.fori_loop(0, F32_EXP_BITS, exponent_bit, thr_key)
    scale = _key_scale(thr_key)
    flip = jnp.where(thr_key < 0, MANT_MASK, 0)

    def mantissa_bit(it, key):
        cand = key + jnp.left_shift(jnp.int32(1), F32_MANT_BITS - 1 - it)
        value = scale * _mantissa_value((cand ^ flip) & MANT_MASK)
        return jnp.where(count_ge(value) >= kf, cand, key)

    thr_key = lax.fori_loop(0, F32_MANT_BITS, mantissa_bit, thr_key)
    thr = _key_value(thr_key)

    cnt_ge = count_ge(thr)
    cnt_gt = count(lambda kk, c: jnp.where(kk > thr, 1.0, 0.0))
    need = kf - cnt_gt
    j_ref[...] = jnp.full((tq, 1), INT_MAX, I32)

    @pl.when(jnp.max(cnt_ge) > kf)
    def _():
        def p3(it, jj):
            cand = jj + jnp.left_shift(jnp.int32(1), idx_bits - 1 - it)
            cnt = count(lambda kk, c: jnp.where(
                kk == thr, jnp.where(lane_pos + c * tk < cand, 1.0, 0.0), 0.0))
            return jnp.where(cnt < need, cand, jj)
        jj = lax.fori_loop(0, idx_bits, p3, jnp.zeros((tq, 1), I32))
        j_ref[...] = jnp.where(cnt_ge > kf, jj, INT_MAX)

    jlast = j_ref[...]

    qscale = HEAD_DIM ** -0.5 * math.log2(math.e)
    lhs = []
    for g in range(N_KV_HEADS):
        blk = jnp.concatenate([q[:, (2 * g) * LANES:(2 * g + 1) * LANES],
                               q[:, (2 * g + 1) * LANES:(2 * g + 2) * LANES]], axis=0)
        lhs.append((blk.astype(F32) * qscale).astype(BF16))
    lo_half = lax.broadcasted_iota(I32, (2 * tq, LANES), 1) < HEAD_DIM

    qmax2 = jnp.zeros((tq, 1), F32)
    for g in range(N_KV_HEADS):
        qf = lhs[g].astype(F32)
        sq = qf * qf
        n2 = jnp.maximum(jnp.sum(jnp.where(lo_half, sq, 0.0), axis=1, keepdims=True),
                         jnp.sum(jnp.where(lo_half, 0.0, sq), axis=1, keepdims=True))
        qmax2 = jnp.maximum(qmax2, jnp.maximum(n2[0:tq], n2[tq:2 * tq]))
    neg_shift = -jnp.sqrt(qmax2 * kmax2)

    def p3b(c, carry):
        kk = key_ref[c]
        kpos = lane_pos + c * tk
        tie = jnp.where(kk == thr, jnp.where(kpos <= jlast, neg_shift, NEG_INF), NEG_INF)
        bias_ref[c] = jnp.where(kpos <= qpos, jnp.where(kk > thr, neg_shift, tie), NEG_INF)
        return carry

    lax.fori_loop(0, nck, p3b, 0)

    def stacks(c):
        start = pl.multiple_of(c * tk, tk)
        b1 = bias_ref[c]
        bias2 = jnp.concatenate([b1, b1], axis=0)
        for g in range(N_KV_HEADS):
            for par in range(2):
                idx = 2 * g + par
                ke = kexp[idx, pl.ds(start, tk), :]
                ve = vexp[idx, pl.ds(start, tk), :]
                yield idx, _dot_nt(lhs[g], ke) + bias2, ve

    def assemble(denoms):
        outs = []
        for g in range(N_KV_HEADS):
            l_lo, l_hi = denoms(g)
            o = (jnp.where(lo_half, acc_ref[2 * g], 0.0) / l_lo
                 + jnp.where(lo_half, 0.0, acc_ref[2 * g + 1]) / l_hi)
            outs.append(o[0:tq])
            outs.append(o[tq:2 * tq])
        o_ref[...] = jnp.concatenate(outs, axis=1).astype(o_ref.dtype)

    def attend_exact():
        m_ref[...] = jnp.full(m_ref.shape, M_INIT, F32)
        l_ref[...] = jnp.zeros(l_ref.shape, F32)
        acc_ref[...] = jnp.zeros(acc_ref.shape, F32)

        def body(c, carry):
            for idx, s, ve in stacks(c):
                m_old = m_ref[idx]
                m_new = jnp.maximum(m_old, jnp.max(s, axis=1, keepdims=True))
                alpha = jnp.exp2(m_old - m_new)
                p = jnp.exp2(s - m_new)
                l_ref[idx] = alpha * l_ref[idx] + jnp.sum(p, axis=1, keepdims=True)
                acc_ref[idx] = alpha * acc_ref[idx] + _dot(p.astype(BF16), ve)
                m_ref[idx] = m_new
            return carry

        lax.fori_loop(0, nck, body, 0)
        assemble(lambda g: (l_ref[2 * g], l_ref[2 * g + 1]))

    acc_ref[...] = jnp.zeros(acc_ref.shape, F32)

    def body(c, carry):
        for idx, s, ve in stacks(c):
            acc_ref[idx] = acc_ref[idx] + _dot(jnp.exp2(s).astype(BF16), ve)
        return carry

    lax.fori_loop(0, nck, body, 0)

    def denoms(g):
        return acc_ref[2 * g][:, HEAD_DIM:HEAD_DIM + 1], acc_ref[2 * g + 1][:, 0:1]

    lmin = jnp.full((2 * tq, 1), jnp.inf, F32)
    for g in range(N_KV_HEADS):
        l_lo, l_hi = denoms(g)
        lmin = jnp.minimum(lmin, jnp.minimum(l_lo, l_hi))
    healthy = jnp.min(lmin) >= L_FLOOR
    assemble(denoms)

    @pl.when(jnp.logical_not(healthy))
    def _():
        attend_exact()


def _attn_scratch(tq, tk, nck_max):
    n_stack = 2 * N_KV_HEADS
    return [pltpu.VMEM((nck_max, tq, tk), F32), pltpu.VMEM((nck_max, tq, tk), F32),
            pltpu.VMEM((n_stack, 2 * tq, 1), F32), pltpu.VMEM((n_stack, 2 * tq, 1), F32),
            pltpu.VMEM((n_stack, 2 * tq, LANES), F32), pltpu.VMEM((tq, 1), I32)]


def _attn_prompt_kernel(q_ref, iq_ref, iw_ref, k_ref, v_ref, ik_ref, o_ref,
                        kexp, vexp, ikexp, kmax_ref, key_ref, bias_ref, m_ref, l_ref, acc_ref, j_ref,
                        *, tk, n_keep, idx_bits):
    i = pl.program_id(1)
    tq = q_ref.shape[0]
    t = k_ref.shape[0]

    @pl.when(i == 0)
    def _():
        kmax_ref[0] = jnp.float32(0.0)

        def fill(c, carry):
            rows = pl.ds(pl.multiple_of(c * tk, tk), tk)
            ke = _expand_pairs(k_ref[rows, :])
            ve = _expand_pairs(v_ref[rows, :], ones_lane=True)
            for s in range(2 * N_KV_HEADS):
                kexp[s, rows, :] = ke[:, s * LANES:(s + 1) * LANES]
                vexp[s, rows, :] = ve[:, s * LANES:(s + 1) * LANES]
            n2 = jnp.zeros((tk, 1), F32)
            for g in range(N_KV_HEADS):
                kf = ke[:, 2 * g * LANES:(2 * g + 1) * LANES].astype(F32)
                n2 = jnp.maximum(n2, jnp.sum(kf * kf, axis=1, keepdims=True))
            kmax_ref[0] = jnp.maximum(kmax_ref[0], jnp.max(n2))
            ik = ik_ref[rows, :].astype(F32)
            ikexp[0, rows, :] = ik.astype(BF16)
            ikexp[1, rows, :] = pltpu.roll(ik, HEAD_DIM, axis=1).astype(BF16)
            return carry
        lax.fori_loop(0, t // tk, fill, 0)

    qpos = i * tq + lax.broadcasted_iota(I32, (tq, 1), 0)
    nck = lax.div((i + 1) * tq - 1, tk) + 1
    _sparse_attend(q_ref[...], iq_ref[...], iw_ref[...], qpos, kexp, vexp, ikexp, nck, tk,
                   n_keep, idx_bits, kmax_ref[0], o_ref, key_ref, bias_ref, m_ref, l_ref, acc_ref, j_ref)


def _attn_prompt(q, iq, iw, k, v, ikb, b, t, tq, tk):
    nq = t // tq
    n_keep = min(TOPK_KEYS, t // 4)
    row = lambda bi, i: (bi * nq + i, 0)
    per_b = lambda bi, i: (bi, 0)
    dq, dk = q.shape[1], k.shape[1]
    kern = functools.partial(_attn_prompt_kernel, tk=tk, n_keep=n_keep,
                             idx_bits=max(1, math.ceil(math.log2(t))))
    return pl.pallas_call(
        kern,
        grid=(b, nq),
        in_specs=[pl.BlockSpec((tq, dq), row), pl.BlockSpec((tq, iq.shape[1]), row),
                  pl.BlockSpec((tq, iw.shape[1]), row),
                  pl.BlockSpec((t, dk), per_b, pipeline_mode=pl.Buffered(1)),
                  pl.BlockSpec((t, dk), per_b, pipeline_mode=pl.Buffered(1)),
                  pl.BlockSpec((t, LANES), per_b, pipeline_mode=pl.Buffered(1))],
        out_specs=pl.BlockSpec((tq, dq), row),
        out_shape=jax.ShapeDtypeStruct((b * t, dq), BF16),
        scratch_shapes=[pltpu.VMEM((2 * N_KV_HEADS, t, LANES), BF16), pltpu.VMEM((2 * N_KV_HEADS, t, LANES), BF16),
                        pltpu.VMEM((2, t, LANES), BF16), pltpu.SMEM((1,), F32)]
        + _attn_scratch(tq, tk, t // tk),
        compiler_params=_cparams(("parallel", "arbitrary")),
        name="attn_prompt",
    )(q, iq, iw, k, v, ikb)


SAMPLE_Q_ROWS = 16


def _entry_rows(src_ref, f32_scr, t_new):
    f32_scr[...] = src_ref[...].astype(F32)
    per_blk = SAMPLE_Q_ROWS // t_new
    off = pl.multiple_of(lax.rem(pl.program_id(0), per_blk) * t_new, t_new)
    return f32_scr[pl.ds(off, t_new), :]


def _sample_geometry(n_pages, page, t_new):
    assert N_HEADS * t_new == LANES and IDX_HEADS * t_new == IDX_DIM and t_new <= page
    past = n_pages * page
    return past, past + page, LANES // t_new


def _with_zero_tail(x, rows):
    return jnp.concatenate([x, jnp.zeros((rows - x.shape[0], x.shape[1]), x.dtype)], axis=0)


def _sample_scores_kernel(pt_ref, iq_ref, w_ref, ikn_ref, *rest, n_pages, t_new, group):
    ikpages = rest[:n_pages]
    o_ref, ikt, iq_scr, ikn_scr = rest[n_pages:]
    bl = lax.rem(pl.program_id(0), group)
    page = ikpages[0].shape[2]
    past = n_pages * page
    for p in range(n_pages):
        ikt[:, p * page:(p + 1) * page] = ikpages[p][0].astype(BF16)
    tail = _with_zero_tail(_entry_rows(ikn_ref, ikn_scr, t_new), page).T
    ikt[:, past:past + page] = tail[0:IDX_DIM].astype(BF16)

    iqf = _entry_rows(iq_ref, iq_scr, t_new)
    pieces = []
    for h in range(IDX_HEADS):
        blk = iqf[:, (h // 2) * LANES:(h // 2 + 1) * LANES]
        if h % 2:
            blk = pltpu.roll(blk, IDX_DIM, axis=1)
        pieces.append(blk[:, :IDX_DIM])
    iqm = jnp.concatenate(pieces, axis=0).astype(BF16)
    r = jnp.maximum(_dot(iqm, ikt[...]), 0.0) * w_ref[0]
    sc = r[0:t_new]
    for h in range(1, IDX_HEADS):
        sc = sc + r[h * t_new:(h + 1) * t_new]
    o_ref[0, pl.ds(pl.multiple_of(bl * t_new, t_new), t_new), :] = sc


def _sample_select_kernel(sc_ref, sel_ref, key_ref, j_ref, *, past, t_new, n_keep, idx_bits):
    rows, s_pad = key_ref.shape
    kpos = lax.broadcasted_iota(I32, (rows, s_pad), 1)
    valid = kpos <= past + lax.rem(lax.broadcasted_iota(I32, (rows, s_pad), 0), t_new)
    key_ref[...] = jnp.where(valid, sc_ref[0], NEG_INF)
    kf = float(n_keep)

    def count(weight):
        return jnp.sum(weight(key_ref[...]), axis=1, keepdims=True)

    def count_ge(value):
        return count(lambda kk: jnp.where(kk >= value, 1.0, 0.0))

    thr_key = jnp.where(count_ge(0.0) >= kf, jnp.zeros((rows, 1), I32), jnp.full((rows, 1), INT_MIN, I32))

    def key_bit(it, key):
        cand = key + jnp.left_shift(jnp.int32(1), 30 - it)
        return jnp.where(count_ge(_key_value(cand)) >= kf, cand, key)

    thr = _key_value(lax.fori_loop(0, F32_EXP_BITS + F32_MANT_BITS, key_bit, thr_key))
    cnt_ge = count_ge(thr)
    need = kf - count(lambda kk: jnp.where(kk > thr, 1.0, 0.0))
    j_ref[...] = jnp.full(j_ref.shape, INT_MAX, I32)

    @pl.when(jnp.max(cnt_ge) > kf)
    def _():
        def p3(it, jj):
            cand = jj + jnp.left_shift(jnp.int32(1), idx_bits - 1 - it)
            cnt = count(lambda kk: jnp.where(kk == thr, jnp.where(kpos < cand, 1.0, 0.0), 0.0))
            return jnp.where(cnt < need, cand, jj)
        jj = lax.fori_loop(0, idx_bits, p3, jnp.zeros((rows, 1), I32))
        j_ref[...] = jnp.where(cnt_ge > kf, jj, INT_MAX)

    jlast = j_ref[...]
    kk = key_ref[...]
    tie = jnp.where(kk == thr, jnp.where(kpos <= jlast, 1.0, 0.0), 0.0)
    sel_ref[0] = jnp.where(valid, jnp.where(kk > thr, 1.0, tie), 0.0)


def _sample_attend_kernel(pt_ref, q_ref, kn_ref, vn_ref, sel_ref, *rest, n_pages, t_new, group):
    kpages = rest[:n_pages]
    vpages = rest[n_pages:2 * n_pages]
    o_ref, kt, vt, q_scr = rest[2 * n_pages:]
    bl = lax.rem(pl.program_id(0), group)
    page = kpages[0].shape[2]
    past = n_pages * page
    dk, s_pad = kt.shape
    for p in range(n_pages):
        cols = slice(p * page, (p + 1) * page)
        kt[:, cols] = kpages[p][0].astype(BF16)
        vt[:, cols] = vpages[p][0].astype(BF16)
    kt[:, past:s_pad] = _with_zero_tail(kn_ref[...], page).T.astype(BF16)
    vt[:, past:s_pad] = _with_zero_tail(vn_ref[...], page).T.astype(BF16)

    qf = _entry_rows(q_ref, q_scr, t_new) * (HEAD_DIM ** -0.5 * math.log2(math.e))
    lo = lax.broadcasted_iota(I32, (t_new, LANES), 1) < HEAD_DIM
    zero = jnp.zeros((t_new, LANES), F32)
    per_group = N_HEADS // N_KV_HEADS
    qrows = []
    for h in range(N_HEADS):
        g = h // per_group
        blk = qf[:, (h // 2) * LANES:(h // 2 + 1) * LANES]
        if h % 2 != g % 2:
            blk = pltpu.roll(blk, HEAD_DIM, axis=1)
        blk = jnp.where(lo, blk, zero) if g % 2 == 0 else jnp.where(lo, zero, blk)
        qrows.append(jnp.concatenate([blk, zero] if g // 2 == 0 else [zero, blk], axis=1))
    qbd = jnp.concatenate(qrows, axis=0).astype(BF16)
    s = _dot(qbd, kt[...])

    picked = sel_ref[0, pl.ds(pl.multiple_of(bl * t_new, t_new), t_new), :]
    s = jnp.where(jnp.concatenate([picked] * N_HEADS, axis=0) > 0.5, s, NEG_INF)
    p = jnp.exp2(s - jnp.max(s, axis=1, keepdims=True))
    o = _dot_nt(p.astype(BF16), vt[...]) / jnp.sum(p, axis=1, keepdims=True)

    outs = []
    for m in range(N_HEADS // 2):
        pair = []
        for h in (2 * m, 2 * m + 1):
            g = h // per_group
            piece = o[h * t_new:(h + 1) * t_new, (g // 2) * LANES:(g // 2 + 1) * LANES]
            pair.append(pltpu.roll(piece, HEAD_DIM, axis=1) if g % 2 != h % 2 else piece)
        outs.append(jnp.where(lo, pair[0], pair[1]))
    o_ref[...] = jnp.concatenate(outs, axis=1)


def _attn_sample(q, iq, iw_col, k_new, v_new, ik_new, cache_kt, cache_vt, cache_ikt, page_table, t_new):
    nb, n_pages = page_table.shape
    n_pool, dk, page = cache_kt.shape
    past, s_pad, group = _sample_geometry(n_pages, page, t_new)
    per_blk = SAMPLE_Q_ROWS // t_new
    assert nb % group == 0 and nb % per_blk == 0 and t_new % SUBLANES == 0
    n_groups = nb // group
    n_keep = min(TOPK_KEYS, (past + t_new) // 4)
    dq = q.shape[1]
    rows = group * t_new
    row = lambda b, pt: (b, 0)
    qrow = lambda b, pt: (b // per_blk, 0)
    grp = lambda b, pt: (b // group, 0, 0)

    def page_spec(height, p):
        return pl.BlockSpec((1, height, page), lambda b, pt, p=p: (pt[b, p], 0, 0))

    scores = pl.pallas_call(
        functools.partial(_sample_scores_kernel, n_pages=n_pages, t_new=t_new, group=group),
        grid_spec=pltpu.PrefetchScalarGridSpec(
            num_scalar_prefetch=1,
            grid=(nb,),
            in_specs=[pl.BlockSpec((SAMPLE_Q_ROWS, iq.shape[1]), qrow),
                      pl.BlockSpec((1, iw_col.shape[1], 1), lambda b, pt: (b, 0, 0)),
                      pl.BlockSpec((SAMPLE_Q_ROWS, ik_new.shape[1]), qrow)]
            + [page_spec(IDX_DIM, p) for p in range(n_pages)],
            out_specs=pl.BlockSpec((1, rows, s_pad), grp),
            scratch_shapes=[pltpu.VMEM((IDX_DIM, s_pad), BF16), pltpu.VMEM((SAMPLE_Q_ROWS, iq.shape[1]), F32),
                            pltpu.VMEM((SAMPLE_Q_ROWS, ik_new.shape[1]), F32)]),
        out_shape=jax.ShapeDtypeStruct((n_groups, rows, s_pad), F32),
        compiler_params=_cparams(("arbitrary",)),
        name="sample_scores",
    )(page_table, iq, iw_col, ik_new, *([cache_ikt] * n_pages))

    sel = pl.pallas_call(
        functools.partial(_sample_select_kernel, past=past, t_new=t_new, n_keep=n_keep,
                          idx_bits=max(1, math.ceil(math.log2(s_pad)))),
        grid=(n_groups,),
        in_specs=[pl.BlockSpec((1, rows, s_pad), lambda g: (g, 0, 0))],
        out_specs=pl.BlockSpec((1, rows, s_pad), lambda g: (g, 0, 0)),
        out_shape=jax.ShapeDtypeStruct((n_groups, rows, s_pad), F32),
        scratch_shapes=[pltpu.VMEM((rows, s_pad), F32), pltpu.VMEM((rows, 1), I32)],
        compiler_params=_cparams(("parallel",)),
        name="sample_select",
    )(scores)

    return pl.pallas_call(
        functools.partial(_sample_attend_kernel, n_pages=n_pages, t_new=t_new, group=group),
        grid_spec=pltpu.PrefetchScalarGridSpec(
            num_scalar_prefetch=1,
            grid=(nb,),
            in_specs=[pl.BlockSpec((SAMPLE_Q_ROWS, dq), qrow),
                      pl.BlockSpec((t_new, dk), row), pl.BlockSpec((t_new, dk), row),
                      pl.BlockSpec((1, rows, s_pad), grp)]
            + [page_spec(dk, p) for p in range(n_pages)] + [page_spec(dk, p) for p in range(n_pages)],
            out_specs=pl.BlockSpec((t_new, dq), row),
            scratch_shapes=[pltpu.VMEM((dk, s_pad), BF16), pltpu.VMEM((dk, s_pad), BF16),
                            pltpu.VMEM((SAMPLE_Q_ROWS, dq), F32)]),
        out_shape=jax.ShapeDtypeStruct((nb * t_new, dq), F32),
        compiler_params=_cparams(("parallel",)),
        name="sample_attend",
    )(page_table, q, k_new, v_new, sel, *([cache_kt] * n_pages), *([cache_vt] * n_pages))


def _layer_norm(x, g, b):
    mu = jnp.mean(x, axis=-1, keepdims=True)
    xc = x - mu
    var = jnp.mean(xc * xc, axis=-1, keepdims=True)
    return xc * lax.rsqrt(var + LN_EPS) * g + b


def _merge_kernel(*refs, alpha, first_tiles):
    wa_ref, wb_ref, wo_ref, g_ref, b_ref, h_ref = refs[10:]

    def run(yr_ref, ya_ref, ga_ref, gb_ref, x_ref):
        pa = _dot(yr_ref[...].astype(BF16), wa_ref[...])
        pb = _dot(ya_ref[...].astype(BF16), wb_ref[...])
        m = jax.nn.sigmoid(ga_ref[...]) * pa + jax.nn.sigmoid(gb_ref[...]) * pb
        mix = _dot(m.astype(BF16), wo_ref[...])
        h_ref[...] = _layer_norm(alpha * x_ref[...] + mix, g_ref[...], b_ref[...])

    @pl.when(pl.program_id(0) < first_tiles)
    def _():
        run(*refs[0:5])

    @pl.when(pl.program_id(0) >= first_tiles)
    def _():
        run(*refs[5:10])


def _merge(first, second, wbr_a, wbr_b, w_out, ln_g, ln_b, alpha, tm):
    n1, d = first[4].shape
    n2 = second[4].shape[0]
    t1, t2 = n1 // tm, n2 // tm
    in_first = lambda i: (jnp.minimum(i, t1 - 1), 0)
    in_second = lambda i: (jnp.maximum(i - t1, 0), 0)
    full = lambda i: (0, 0)
    return pl.pallas_call(
        functools.partial(_merge_kernel, alpha=alpha, first_tiles=t1),
        grid=(t1 + t2,),
        in_specs=[pl.BlockSpec((tm, d), in_first)] * 5 + [pl.BlockSpec((tm, d), in_second)] * 5
        + [pl.BlockSpec((d, d), full)] * 3 + [pl.BlockSpec((1, d), full)] * 2,
        out_specs=pl.BlockSpec((tm, d), lambda i: (i, 0)),
        out_shape=jax.ShapeDtypeStruct((n1 + n2, d), F32),
        compiler_params=_cparams(("arbitrary",)),
        name="merge_ln1",
    )(*first, *second, wbr_a, wbr_b, w_out, ln_g, ln_b)


def _router_kernel(h_ref, whi_ref, wlo_ref, bias_ref, e_ref, g_ref, r_ref, cnt_ref, run_ref):
    h = h_ref[...]
    tm = h.shape[0]
    ne = whi_ref.shape[1]
    h_hi = h.astype(BF16)
    h_lo = (h - h_hi.astype(F32)).astype(BF16)
    logits = _dot(h_hi, whi_ref[...]) + (_dot(h_lo, whi_ref[...]) + _dot(h_hi, wlo_ref[...]))
    scores = jax.nn.sigmoid(logits)
    biased = scores + bias_ref[...]
    lane = lax.broadcasted_iota(I32, (tm, ne), 1)
    lane_f = lane.astype(F32)
    per_group = ne // N_EXPERT_GROUPS
    big = float(ne)

    def first_argmax(v):
        m = jnp.max(v, axis=1, keepdims=True)
        idx = jnp.min(jnp.where(v == m, lane_f, big), axis=1, keepdims=True)
        return m, idx

    gscore = []
    for g in range(N_EXPERT_GROUPS):
        in_g = (lane >= g * per_group) & (lane < (g + 1) * per_group)
        mg = jnp.where(in_g, biased, NEG_INF)
        m1, i1 = first_argmax(mg)
        m2 = jnp.max(jnp.where(lane_f == i1, NEG_INF, mg), axis=1, keepdims=True)
        gscore.append(m1 + m2)

    ok_map = jnp.zeros((tm, ne), F32)
    for g in range(N_EXPERT_GROUPS):
        rank = jnp.zeros((tm, 1), F32)
        for o in range(N_EXPERT_GROUPS):
            if o == g:
                continue
            ahead = (gscore[o] > gscore[g]) if o > g else (gscore[o] >= gscore[g])
            rank = rank + jnp.where(ahead, 1.0, 0.0)
        in_g = (lane >= g * per_group) & (lane < (g + 1) * per_group)
        ok_map = jnp.where(in_g, jnp.where(rank < float(TOPK_GROUPS), 1.0, 0.0), ok_map)

    cur = jnp.where(ok_map > 0.5, biased, NEG_INF)
    out_lane = lax.broadcasted_iota(I32, (tm, LANES), 1)
    e_out = jnp.zeros((tm, LANES), F32)
    s_out = jnp.zeros((tm, LANES), F32)
    total = jnp.zeros((tm, 1), F32)
    picked = jnp.zeros((tm, ne), F32)
    hits = []
    for j in range(EXPERT_TOP_K):
        _, idx = first_argmax(cur)
        hit = lane_f == idx
        hits.append(hit)
        sel = jnp.sum(jnp.where(hit, scores, 0.0), axis=1, keepdims=True)
        cur = jnp.where(hit, NEG_INF, cur)
        picked = jnp.where(hit, 1.0, picked)
        e_out = jnp.where(out_lane == j, idx, e_out)
        s_out = jnp.where(out_lane == j, sel, s_out)
        total = total + sel
    e_ref[...] = e_out
    g_ref[...] = ROUTED_SCALE * s_out / total

    @pl.when(pl.program_id(0) == 0)
    def _():
        run_ref[...] = jnp.zeros_like(run_ref)

    pk = picked.astype(BF16)
    r_i = lax.broadcasted_iota(I32, (tm, tm), 0)
    c_i = lax.broadcasted_iota(I32, (tm, tm), 1)
    before = _dot(jnp.where(c_i < r_i, 1.0, 0.0).astype(BF16), pk) + run_ref[0:1, :]
    r_out = jnp.zeros((tm, LANES), F32)
    for j in range(EXPERT_TOP_K):
        rank = jnp.sum(jnp.where(hits[j], before, 0.0), axis=1, keepdims=True)
        r_out = jnp.where(out_lane == j, rank, r_out)
    r_ref[...] = r_out
    run_ref[...] = run_ref[...] + _dot(jnp.ones((SUBLANES, tm), BF16), pk)
    cnt_ref[...] = run_ref[...]


def _router(h, w_hi, w_lo, bias, tm):
    n, d = h.shape
    ne = w_hi.shape[1]
    row = lambda i: (i, 0)
    full = lambda i: (0, 0)
    return pl.pallas_call(
        _router_kernel,
        grid=(n // tm,),
        in_specs=[pl.BlockSpec((tm, d), row), pl.BlockSpec((d, ne), full), pl.BlockSpec((d, ne), full),
                  pl.BlockSpec((1, ne), full)],
        out_specs=(pl.BlockSpec((tm, LANES), row), pl.BlockSpec((tm, LANES), row),
                   pl.BlockSpec((tm, LANES), row), pl.BlockSpec((SUBLANES, ne), full)),
        out_shape=(jax.ShapeDtypeStruct((n, LANES), F32), jax.ShapeDtypeStruct((n, LANES), F32),
                   jax.ShapeDtypeStruct((n, LANES), F32), jax.ShapeDtypeStruct((SUBLANES, ne), F32)),
        scratch_shapes=[pltpu.VMEM((SUBLANES, ne), F32)],
        compiler_params=_cparams(("arbitrary",)),
        name="router",
    )(h, w_hi, w_lo, bias)


def _row_layout(counts, n_rows):
    n_experts = counts.shape[0]
    n_blocks = n_rows // MOE_BLOCK
    padded = (counts + MOE_BLOCK - 1) // MOE_BLOCK * MOE_BLOCK
    pad_end = jnp.cumsum(padded)
    pad_start = pad_end - padded
    block_start = jnp.arange(n_blocks, dtype=I32) * MOE_BLOCK
    block_expert = jnp.minimum(jnp.sum((pad_end[None, :] <= block_start[:, None]).astype(I32), axis=1),
                               n_experts - 1).astype(I32)
    n_active = (pad_end[-1] // MOE_BLOCK).astype(I32).reshape(1)
    ids = jnp.arange(n_experts, dtype=I32)
    live = counts > 0
    later = jnp.flip(lax.cummin(jnp.flip(jnp.where(live, ids, n_experts))))
    nxt = jnp.concatenate([later[1:], jnp.full((1,), n_experts, I32)])
    next_expert = jnp.where(nxt < n_experts, nxt, -1).astype(I32)
    slot_of = ((jnp.cumsum(live.astype(I32)) - 1) & 1).astype(I32)
    return pad_start.astype(I32), pad_end.astype(I32), block_expert, n_active, next_expert, slot_of


def _dest_kernel(e_ref, r_ref, ps_ref, d_ref):
    e = e_ref[...]
    tm = e.shape[0]
    ne = ps_ref.shape[1]
    lane_f = lax.broadcasted_iota(I32, (tm, ne), 1).astype(F32)
    out_lane = lax.broadcasted_iota(I32, (tm, LANES), 1)
    ps = ps_ref[...]
    out = r_ref[...]
    for j in range(EXPERT_TOP_K):
        start = jnp.sum(jnp.where(lane_f == e[:, j:j + 1], ps, 0.0), axis=1, keepdims=True)
        out = jnp.where(out_lane == j, out + start, out)
    d_ref[...] = out.astype(I32)


def _dest(e_f, r_f, pad_start_f, tm):
    n = e_f.shape[0]
    ne = pad_start_f.shape[1]
    row = lambda i: (i, 0)
    return pl.pallas_call(
        _dest_kernel,
        grid=(n // tm,),
        in_specs=[pl.BlockSpec((tm, LANES), row), pl.BlockSpec((tm, LANES), row),
                  pl.BlockSpec((1, ne), lambda i: (0, 0))],
        out_specs=pl.BlockSpec((tm, LANES), row),
        out_shape=jax.ShapeDtypeStruct((n, LANES), I32),
        compiler_params=_cparams(("parallel",)),
        name="moe_dest",
    )(e_f, r_f, pad_start_f)


def _pow2_below(n):
    return [1 << b for b in range(n.bit_length() - 1, -1, -1)]


def _scatter_kernel(cnt_ref, ps_ref, pe_ref, dest_ref, h_ref, xs_hbm, zbuf, sem, zsem, *, top_k, n_rows):
    i = pl.program_id(0)
    tm = h_ref.shape[0]
    n_experts = cnt_ref.shape[0]

    def row_copy(t, j):
        return pltpu.make_async_copy(h_ref.at[pl.ds(t, 1), :],
                                     xs_hbm.at[pl.ds(dest_ref[0, 0, t * top_k + j], 1), :], sem)

    def issue(t, carry):
        for j in range(top_k):
            row_copy(t, j).start(priority=j % 2)
        return carry

    lax.fori_loop(0, tm, issue, 0)

    def zero_copy(start, size):
        return pltpu.make_async_copy(zbuf.at[pl.ds(0, size), :], xs_hbm.at[pl.ds(start, size), :], zsem)

    @pl.when(i == 0)
    def _():
        zbuf[...] = jnp.zeros_like(zbuf)

        def fill(first, end, wait):
            aligned = (first + (SUBLANES - 1)) & ~(SUBLANES - 1)
            for s in range(SUBLANES - 1):
                @pl.when(first + s < aligned)
                def _():
                    cp = zero_copy(first + s, 1)
                    cp.wait() if wait else cp.start()
            n_tiles = lax.div(end - aligned, jnp.int32(SUBLANES))
            for size in _pow2_below(MOE_BLOCK // SUBLANES - 1):
                @pl.when((n_tiles & size) != 0)
                def _():
                    start = aligned + (n_tiles & ~(2 * size - 1)) * SUBLANES
                    cp = zero_copy(pl.multiple_of(start, SUBLANES), size * SUBLANES)
                    cp.wait() if wait else cp.start()

        def per_expert(wait):
            def body(e, carry):
                fill(ps_ref[e] + cnt_ref[e], pe_ref[e], wait)
                return carry
            lax.fori_loop(0, n_experts, body, 0)

        def tail(wait):
            total = pe_ref[n_experts - 1]

            def body(b, carry):
                cp = zero_copy(pl.multiple_of(total + b * MOE_BLOCK, MOE_BLOCK), MOE_BLOCK)
                cp.wait() if wait else cp.start()
                return carry
            lax.fori_loop(0, lax.div(n_rows - total, jnp.int32(MOE_BLOCK)), body, 0)

        per_expert(False)
        tail(False)
        per_expert(True)
        tail(True)

    pltpu.make_async_copy(h_ref, xs_hbm.at[pl.ds(0, tm), :], sem).wait()
    for _ in range(top_k - 1):
        pltpu.make_async_copy(h_ref, xs_hbm.at[pl.ds(0, tm), :], sem).wait()


def _scatter_rows(h, dest, counts, pad_start, pad_end, n_rows, top_k, tm):
    n, d = h.shape
    grid_spec = pltpu.PrefetchScalarGridSpec(
        num_scalar_prefetch=3,
        grid=(n // tm,),
        in_specs=[pl.BlockSpec((1, 1, tm * top_k), lambda i, *_: (i, 0, 0), memory_space=pltpu.SMEM),
                  pl.BlockSpec((tm, d), lambda i, *_: (i, 0))],
        out_specs=pl.BlockSpec(memory_space=pl.ANY),
        scratch_shapes=[pltpu.VMEM((MOE_BLOCK, d), F32), pltpu.SemaphoreType.DMA(()),
                        pltpu.SemaphoreType.DMA(())])
    return pl.pallas_call(
        functools.partial(_scatter_kernel, top_k=top_k, n_rows=n_rows),
        grid_spec=grid_spec,
        out_shape=jax.ShapeDtypeStruct((n_rows, d), F32),
        compiler_params=_cparams(("arbitrary",)),
        name="moe_scatter",
    )(counts, pad_start, pad_end, dest.reshape(n // tm, 1, tm * top_k), h)


def _experts_kernel(be_ref, na_ref, nxt_ref, slot_ref, x_ref, wg_hbm, wu_hbm, wd_hbm, o_ref,
                    wg_f, wu_f, wd_f, wg_b, wu_b, wd_b, sem):
    i = pl.program_id(0)

    def fetch(e, slot):
        return (pltpu.make_async_copy(wg_hbm.at[e], wg_f.at[slot], sem.at[slot, 0]),
                pltpu.make_async_copy(wu_hbm.at[e], wu_f.at[slot], sem.at[slot, 1]),
                pltpu.make_async_copy(wd_hbm.at[e], wd_f.at[slot], sem.at[slot, 2]))

    @pl.when(i < na_ref[0])
    def _():
        e = be_ref[i]
        slot = slot_ref[e]

        @pl.when(i == 0)
        def _():
            for cp in fetch(e, slot):
                cp.start()

        @pl.when((i == 0) | (be_ref[jnp.maximum(i - 1, 0)] != e))
        def _():
            for cp in fetch(e, slot):
                cp.wait()
            nxt = nxt_ref[e]

            @pl.when(nxt >= 0)
            def _():
                for cp in fetch(nxt, 1 - slot):
                    cp.start()

            wg_b[...] = wg_f[slot].astype(BF16)
            wu_b[...] = wu_f[slot].astype(BF16)
            wd_b[...] = wd_f[slot].astype(BF16)

        xb = x_ref[...].astype(BF16)
        g = _dot(xb, wg_b[...])
        u = _dot(xb, wu_b[...])
        hb = (g * jax.nn.sigmoid(g)) * u
        o_ref[...] = _dot(hb.astype(BF16), wd_b[...])

    @pl.when(i >= na_ref[0])
    def _():
        o_ref[...] = jnp.zeros(o_ref.shape, o_ref.dtype)


def _experts(xs, block_expert, n_active, next_expert, slot_of, w_gate, w_up, w_down):
    n_rows, d = xs.shape
    ne, _, de = w_gate.shape
    n_blocks = n_rows // MOE_BLOCK
    live = lambda i, be, na, nx, sl: (jnp.minimum(i, na[0] - 1), 0)
    grid_spec = pltpu.PrefetchScalarGridSpec(
        num_scalar_prefetch=4,
        grid=(n_blocks,),
        in_specs=[pl.BlockSpec((MOE_BLOCK, d), live)] + [pl.BlockSpec(memory_space=pl.ANY)] * 3,
        out_specs=pl.BlockSpec((MOE_BLOCK, d), lambda i, be, na, nx, sl: (i, 0)),
        scratch_shapes=[pltpu.VMEM((2, d, de), F32), pltpu.VMEM((2, d, de), F32), pltpu.VMEM((2, de, d), F32),
                        pltpu.VMEM((d, de), BF16), pltpu.VMEM((d, de), BF16), pltpu.VMEM((de, d), BF16),
                        pltpu.SemaphoreType.DMA((2, 3))])
    return pl.pallas_call(
        _experts_kernel,
        grid_spec=grid_spec,
        out_shape=jax.ShapeDtypeStruct((n_rows, d), F32),
        compiler_params=_cparams(("arbitrary",)),
        name="moe_experts",
    )(block_expert, n_active, next_expert, slot_of, xs, w_gate, w_up, w_down)


def _combine_kernel(dest_ref, rows_hbm, h_ref, gate_ref, wsg_ref, wsu_ref, wsd_ref, g_ref, b_ref, y_ref,
                    gbuf, sem, *, alpha, top_k):
    tm = h_ref.shape[0]

    def issue(t, carry):
        for j in range(top_k):
            pltpu.make_async_copy(rows_hbm.at[pl.ds(dest_ref[0, 0, t * top_k + j], 1), :],
                                  gbuf.at[j, pl.ds(t, 1), :], sem).start(priority=j % 2)
        return carry

    lax.fori_loop(0, tm, issue, 0)
    h = h_ref[...]
    hb = h.astype(BF16)
    sg = _dot(hb, wsg_ref[...])
    su = _dot(hb, wsu_ref[...])
    shared = _dot(((sg * jax.nn.sigmoid(sg)) * su).astype(BF16), wsd_ref[...])
    gates = gate_ref[...]
    for j in range(top_k):
        pltpu.make_async_copy(rows_hbm.at[pl.ds(0, tm), :], gbuf.at[j], sem).wait()
    routed = gbuf[0] * gates[:, 0:1]
    for j in range(1, top_k):
        routed = routed + gbuf[j] * gates[:, j:j + 1]
    y_ref[...] = _layer_norm(alpha * h + (routed + shared), g_ref[...], b_ref[...])


def _combine(h, rows, dest, gates, wsg, wsu, wsd, ln_g, ln_b, alpha, top_k, tm):
    n, d = h.shape
    ds_ = wsg.shape[1]
    row = lambda i: (i, 0)
    full = lambda i: (0, 0)
    return pl.pallas_call(
        functools.partial(_combine_kernel, alpha=alpha, top_k=top_k),
        grid=(n // tm,),
        in_specs=[pl.BlockSpec((1, 1, tm * top_k), lambda i: (i, 0, 0), memory_space=pltpu.SMEM),
                  pl.BlockSpec(memory_space=pl.ANY),
                  pl.BlockSpec((tm, d), row), pl.BlockSpec((tm, LANES), row),
                  pl.BlockSpec((d, ds_), full), pl.BlockSpec((d, ds_), full), pl.BlockSpec((ds_, d), full),
                  pl.BlockSpec((1, d), full), pl.BlockSpec((1, d), full)],
        out_specs=pl.BlockSpec((tm, d), row),
        out_shape=jax.ShapeDtypeStruct((n, d), F32),
        scratch_shapes=[pltpu.VMEM((top_k, tm, d), F32), pltpu.SemaphoreType.DMA(())],
        compiler_params=_cparams(("arbitrary",)),
        name="moe_combine",
    )(dest.reshape(n // tm, 1, tm * top_k), rows, h, gates, wsg, wsu, wsd, ln_g, ln_b)


def _moe(h, w_r_hi, w_r_lo, r_bias, w_gate, w_up, w_down, wsg, wsu, wsd, ln_g, ln_b, alpha):
    n, d = h.shape
    ne = w_r_hi.shape[1]
    top_k = EXPERT_TOP_K
    n_rows = -(-(n * top_k + ne * (MOE_BLOCK - 1)) // MOE_BLOCK) * MOE_BLOCK
    e_f, g_f, r_f, cnt = _router(h, w_r_hi, w_r_lo, r_bias, 256)
    counts = cnt[0].astype(I32)
    pad_start, pad_end, block_expert, n_active, next_expert, slot_of = _row_layout(counts, n_rows)
    dest = _dest(e_f, r_f, pad_start.astype(F32).reshape(1, ne), 256)[:, :top_k]
    xs = _scatter_rows(h, dest, counts, pad_start, pad_end, n_rows, top_k, 128)
    ys = _experts(xs, block_expert, n_active, next_expert, slot_of, w_gate, w_up, w_down)
    return _combine(h, ys, dest, g_f, wsg, wsu, wsd, ln_g, ln_b, alpha, top_k, 128)


def _layer(xp, xs, cache_k, cache_v, cache_idx_k, state_conv, state_h, page_table, depth,
           w_in, conv_w, conv_b, w_a, b_a, w_i, b_i, lru_lambda, w_branch, w_out, ln1_g, ln1_b,
           w_router, router_bias, w_exp_gate, w_exp_up, w_exp_down, w_sh_gate, w_sh_up, w_sh_down,
           ln2_g, ln2_b):
    bp, tp, d = xp.shape
    bs, ts, _ = xs.shape
    n_pages = page_table.shape[1]
    past = n_pages * PAGE_SIZE
    alpha = (2.0 * depth) ** 0.25
    d_rnn = conv_w.shape[1]
    dkv = N_KV_HEADS * HEAD_DIM

    w_packed = _pack_w_in(w_in)
    lru_w = _prep_lru(conv_w, conv_b, w_a, b_a, w_i, b_i, lru_lambda)

    np_ = bp * tp
    tabs_p = _rope_tables(jnp.arange(tp, dtype=I32))
    tm = 256
    (xr, gr, q, k_p, v_p, iq, ik_p, ikb, iw, ga_p, gb_p) = _project(
        xp.reshape(np_, d), w_packed, tabs_p, tp // tm, tm)
    y_rnn_p, h_p = _rglru_prompt(xr, gr, bp, tp, lru_w, 256)
    conv_p = xr.reshape(bp, tp, d_rnn)[:, tp - (CONV_W - 1):, :]
    y_att_p = _attn_prompt(q, iq, iw, k_p, v_p, ikb, bp, tp, 256, 512)

    ns = bs * ts
    pos_s = past + jnp.arange(ts, dtype=I32)
    tabs_s = tuple(jnp.tile(t_, (ns // ts, 1)) for t_ in _rope_tables(pos_s))
    (xr_s, gr_s, q_s, k_s, v_s, iq_s, ik_s, ikb_s, iw_s, ga_s, gb_s) = _project(
        xs.reshape(ns, d), w_packed, tabs_s, ns // tm, tm)
    xp_s = jnp.concatenate([state_conv.astype(F32), xr_s.reshape(bs, ts, d_rnn)], axis=1)
    y_rnn_s, hs_s = _rglru_sample(xp_s, gr_s.reshape(bs, ts, d_rnn), state_h.reshape(bs, 1, d_rnn), lru_w, 16)
    conv_s = xp_s[:, ts:, :]
    h_s = hs_s[:, ts - 1, :]

    n_pool = cache_k.shape[0]
    iw_col = iw_s.reshape(bs, ts, IDX_HEADS).transpose(0, 2, 1).reshape(bs, IDX_HEADS * ts, 1)
    key_minor = lambda c: jnp.transpose(c, (0, 2, 3, 1)).reshape(n_pool, dkv, PAGE_SIZE)
    y_att_s = _attn_sample(q_s, iq_s, iw_col, k_s, v_s, ikb_s, key_minor(cache_k), key_minor(cache_v),
                           jnp.transpose(cache_idx_k, (0, 2, 1)), page_table, ts)

    wbr = w_branch.astype(BF16)
    h1 = _merge((y_rnn_p, y_att_p, ga_p, gb_p, xp.reshape(np_, d)),
                (y_rnn_s.reshape(ns, d_rnn), y_att_s, ga_s, gb_s, xs.reshape(ns, d)),
                wbr[:d_rnn], wbr[d_rnn:], w_out.astype(BF16),
                ln1_g.reshape(1, d), ln1_b.reshape(1, d), alpha, 256)

    w_r = w_router.astype(F32)
    w_r_hi = w_r.astype(BF16)
    w_r_lo = (w_r - w_r_hi.astype(F32)).astype(BF16)
    ne = w_router.shape[1]
    y = _moe(h1, w_r_hi, w_r_lo, router_bias.reshape(1, ne).astype(F32), w_exp_gate, w_exp_up, w_exp_down,
             w_sh_gate.astype(BF16), w_sh_up.astype(BF16), w_sh_down.astype(BF16),
             ln2_g.reshape(1, d), ln2_b.reshape(1, d), alpha)

    yp = y[:np_].reshape(bp, tp, d)
    ys = y[np_:].reshape(bs, ts, d)
    st = (k_p.reshape(bp, tp, N_KV_HEADS, HEAD_DIM), v_p.reshape(bp, tp, N_KV_HEADS, HEAD_DIM),
          ik_p.reshape(bp, tp, IDX_DIM), conv_p, h_p,
          k_s.reshape(bs, ts, N_KV_HEADS, HEAD_DIM), v_s.reshape(bs, ts, N_KV_HEADS, HEAD_DIM),
          ik_s.reshape(bs, ts, IDX_DIM), conv_s, h_s)
    return yp, ys, st


def kernel(x_prompt, x_sample, cache_k, cache_v, cache_idx_k, state_conv, state_h, page_table, w_in, conv_w,
           conv_b, w_a, b_a, w_i, b_i, lru_lambda, w_branch, w_out, ln1_g, ln1_b, w_router, router_bias,
           w_exp_gate, w_exp_up, w_exp_down, w_sh_gate, w_sh_up, w_sh_down, ln2_g, ln2_b):
    depth = w_in.shape[0]
    yp, ys = x_prompt, x_sample
    states = []
    for l in range(depth):
        yp, ys, st = _layer(
            yp, ys, cache_k[l], cache_v[l], cache_idx_k[l], state_conv[l], state_h[l], page_table, depth,
            w_in[l], conv_w[l], conv_b[l], w_a[l], b_a[l], w_i[l], b_i[l], lru_lambda[l], w_branch[l], w_out[l],
            ln1_g[l], ln1_b[l], w_router[l], router_bias[l], w_exp_gate[l], w_exp_up[l], w_exp_down[l],
            w_sh_gate[l], w_sh_up[l], w_sh_down[l], ln2_g[l], ln2_b[l])
        states.append(st)
    stacked = [jnp.stack(c) for c in zip(*states)]
    return (yp, ys, *stacked)
```

```python
import functools
import math

import jax
import jax.numpy as jnp
from jax import lax
from jax.experimental import pallas as pl
from jax.experimental.pallas import tpu as pltpu

F32 = jnp.float32
BF16 = jnp.bfloat16
I32 = jnp.int32

LRU_BLOCKS = 16
CONV_W = 4
LRU_C = 8.0
N_HEADS = 16
N_KV_HEADS = 4
HEAD_DIM = 64
ROT_DIM = 16
ROPE_THETA = 500000.0
IDX_HEADS = 8
IDX_DIM = 64
TOPK_KEYS = 256
PAGE_SIZE = 128
N_EXPERT_GROUPS = 8
TOPK_GROUPS = 4
EXPERT_TOP_K = 8
ROUTED_SCALE = 2.5
MOE_BLOCK = 256
LN_EPS = 1e-5

LANES = 128
SUBLANES = 8
VMEM_LIMIT = 56 * 1024 * 1024

NEG_INF = float("-inf")


def _cparams(sem):
    return pltpu.CompilerParams(dimension_semantics=sem, vmem_limit_bytes=VMEM_LIMIT)


def _dot(a, b):
    return jnp.dot(a, b, preferred_element_type=F32)


def _dot_nt(a, b):
    return lax.dot_general(a, b, (((1,), (1,)), ((), ())), preferred_element_type=F32)


_PROJ_GROUPS = (
    ("xr", 1024, 1024, False),
    ("gr", 1024, 1024, False),
    ("q", 1024, 1024, True),
    ("k", 256, 256, True),
    ("v", 256, 256, False),
    ("iq", 512, 512, True),
    ("ik", 64, 128, True),
    ("iw", 8, 128, False),
    ("ga", 1024, 1024, False),
    ("gb", 1024, 1024, False),
)


def _pack_w_in(w_in):
    cols = []
    c0 = 0
    for _, w, wp, _ in _PROJ_GROUPS:
        blk = w_in[:, c0:c0 + w]
        if wp != w:
            blk = jnp.pad(blk, ((0, 0), (0, wp - w)))
        cols.append(blk)
        c0 += w
    return jnp.concatenate(cols, axis=1).astype(BF16)


def _rope_tables(pos):
    half = ROT_DIM // 2
    inv_freq = jnp.power(ROPE_THETA, -jnp.arange(half, dtype=F32) * (2.0 / ROT_DIM))
    ang = pos.astype(F32)[:, None] * inv_freq[None, :]
    cos = jnp.cos(ang)
    sin = jnp.sin(ang)
    t = pos.shape[0]
    rest = HEAD_DIM - ROT_DIM
    c = jnp.concatenate([cos, cos, jnp.ones((t, rest), F32)], axis=1)
    sa = jnp.concatenate([-sin, jnp.zeros((t, half + rest), F32)], axis=1)
    sb = jnp.concatenate([jnp.zeros((t, half), F32), sin, jnp.zeros((t, rest), F32)], axis=1)
    rep = LANES // HEAD_DIM
    return jnp.tile(c, (1, rep)), jnp.tile(sa, (1, rep)), jnp.tile(sb, (1, rep))


def _proj_kernel(x_ref, w_ref, c_ref, sa_ref, sb_ref,
                 xr_ref, gr_ref, q_ref, k_ref, v_ref, iq_ref,
                 ik_ref, ikb_ref, iw_ref, ga_ref, gb_ref, *, iw_scale):
    xb = x_ref[...].astype(BF16)
    c = c_ref[...]
    sa = sa_ref[...]
    sb = sb_ref[...]

    def rope(z):
        n = z.shape[1] // LANES
        cc = jnp.concatenate([c] * n, axis=1) if n > 1 else c
        aa = jnp.concatenate([sa] * n, axis=1) if n > 1 else sa
        bb = jnp.concatenate([sb] * n, axis=1) if n > 1 else sb
        half = ROT_DIM // 2
        up = pltpu.roll(z, z.shape[1] - half, axis=1)
        dn = pltpu.roll(z, half, axis=1)
        return z * cc + up * aa + dn * bb

    c0 = 0
    zs = {}
    for name, _, wp, rot in _PROJ_GROUPS:
        z = _dot(xb, w_ref[:, c0:c0 + wp])
        zs[name] = rope(z) if rot else z
        c0 += wp
    xr_ref[...] = zs["xr"]
    gr_ref[...] = zs["gr"]
    q_ref[...] = zs["q"].astype(BF16)
    k_ref[...] = zs["k"]
    v_ref[...] = zs["v"]
    iq_ref[...] = zs["iq"].astype(BF16)
    ik_ref[...] = zs["ik"][:, :IDX_DIM]
    ikb_ref[...] = zs["ik"].astype(BF16)
    iw_ref[...] = zs["iw"][:, :IDX_HEADS] * iw_scale
    ga_ref[...] = zs["ga"]
    gb_ref[...] = zs["gb"]


def _project(x2d, w_packed, tabs, n_tab_blocks, tm):
    m, d = x2d.shape
    n_total = w_packed.shape[1]
    grid = (m // tm,)
    row = lambda i: (i, 0)
    tab = lambda i: (i % n_tab_blocks, 0)
    out_defs = (
        (1024, F32), (1024, F32), (1024, BF16), (256, F32), (256, F32),
        (512, BF16), (IDX_DIM, F32), (LANES, BF16), (IDX_HEADS, F32), (1024, F32), (1024, F32))
    out_shape = tuple(jax.ShapeDtypeStruct((m, w), dt) for w, dt in out_defs)
    out_specs = tuple(pl.BlockSpec((tm, w), row) for w, _ in out_defs)
    return pl.pallas_call(
        functools.partial(_proj_kernel, iw_scale=IDX_HEADS ** -0.5 * IDX_DIM ** -0.5),
        grid=grid,
        in_specs=[pl.BlockSpec((tm, d), row),
                  pl.BlockSpec((d, n_total), lambda i: (0, 0)),
                  pl.BlockSpec((tm, LANES), tab),
                  pl.BlockSpec((tm, LANES), tab),
                  pl.BlockSpec((tm, LANES), tab)],
        out_specs=out_specs,
        out_shape=out_shape,
        compiler_params=_cparams(("parallel",)),
        name="in_proj",
    )(x2d, w_packed, *tabs)


def _block_diag(w, per):
    n, d, _ = w.shape
    g = n // per
    w4 = w.reshape(g, per, d, d)
    eye = jnp.eye(per, dtype=w.dtype)
    out = jnp.einsum("gpde,pq->gpdqe", w4, eye)
    return out.reshape(g, per * d, per * d)


def _gelu_tanh(x):
    return 0.5 * x * (1.0 + jnp.tanh(math.sqrt(2.0 / math.pi) * (x + 0.044715 * (x * x * x))))


def _lru_gates(xc, wa_ref, ba_ref, wi_ref, bi_ref, clam_ref):
    xb = xc.astype(BF16)
    n_tiles = wa_ref.shape[0]
    tw = wa_ref.shape[1]
    ra, ri = [], []
    for g in range(n_tiles):
        xs = xb[:, g * tw:(g + 1) * tw]
        ra.append(_dot(xs, wa_ref[g]))
        ri.append(_dot(xs, wi_ref[g]))
    r = jax.nn.sigmoid(jnp.concatenate(ra, axis=1) + ba_ref[...])
    ig = jax.nn.sigmoid(jnp.concatenate(ri, axis=1) + bi_ref[...])
    log_a = clam_ref[...] * r
    a = jnp.exp(log_a)
    u = jnp.sqrt(-jnp.tanh(log_a) * (a * a + 1.0)) * (ig * xc)
    return a, u


def _scan8(a3, u3):
    t_idx = lax.broadcasted_iota(I32, a3.shape, 1)
    for d in (1, 2, 4):
        keep = t_idx >= d
        a_sh = jnp.where(keep, pltpu.roll(a3, d, axis=1), 1.0)
        u_sh = jnp.where(keep, pltpu.roll(u3, d, axis=1), 0.0)
        u3 = a3 * u_sh + u3
        a3 = a3 * a_sh
    return a3, u3


def _rglru_prompt_kernel(xr_ref, gr_ref, cw_ref, cb_ref, wa_ref, ba_ref, wi_ref, bi_ref, clam_ref,
                         y_ref, hl_ref, xbuf, hc):
    i = pl.program_id(1)
    tt, d = xr_ref.shape

    @pl.when(i == 0)
    def _():
        xbuf[0:SUBLANES, :] = jnp.zeros((SUBLANES, d), F32)
        hc[...] = jnp.zeros_like(hc)

    x = xr_ref[...]
    xbuf[SUBLANES:SUBLANES + tt, :] = x
    xc = cb_ref[...] + cw_ref[CONV_W - 1:CONV_W, :] * x
    for j in range(CONV_W - 1):
        back = CONV_W - 1 - j
        xc = xc + cw_ref[j:j + 1, :] * xbuf[SUBLANES - back:SUBLANES - back + tt, :]
    xbuf[0:SUBLANES, :] = x[tt - SUBLANES:tt, :]

    a, u = _lru_gates(xc, wa_ref, ba_ref, wi_ref, bi_ref, clam_ref)
    g = tt // SUBLANES
    a3, u3 = _scan8(a.reshape(g, SUBLANES, d), u.reshape(g, SUBLANES, d))
    h_in = hc[...]
    hs = []
    for gi in range(g):
        h8 = a3[gi] * h_in + u3[gi]
        hs.append(h8)
        h_in = jnp.broadcast_to(h8[SUBLANES - 1:SUBLANES, :], (SUBLANES, d))
    hc[...] = h_in
    h = jnp.concatenate(hs, axis=0)
    y_ref[...] = (h * _gelu_tanh(gr_ref[...])).astype(y_ref.dtype)

    @pl.when(i == pl.num_programs(1) - 1)
    def _():
        hl_ref[0] = h_in


def _rglru_prompt(xr, gr, b, t, lru_w, tt):
    d = xr.shape[1]
    nt = t // tt
    row = lambda bi, i: (bi * nt + i, 0)
    full2 = lambda bi, i: (0, 0)
    full3 = lambda bi, i: (0, 0, 0)
    cw, cb, wa, ba, wi, bi_, clam = lru_w
    y, hl = pl.pallas_call(
        _rglru_prompt_kernel,
        grid=(b, nt),
        in_specs=[pl.BlockSpec((tt, d), row), pl.BlockSpec((tt, d), row),
                  pl.BlockSpec(cw.shape, full2), pl.BlockSpec(cb.shape, full2),
                  pl.BlockSpec(wa.shape, full3), pl.BlockSpec(ba.shape, full2),
                  pl.BlockSpec(wi.shape, full3), pl.BlockSpec(bi_.shape, full2),
                  pl.BlockSpec(clam.shape, full2)],
        out_specs=(pl.BlockSpec((tt, d), row),
                   pl.BlockSpec((1, SUBLANES, d), lambda bi, i: (bi, 0, 0))),
        out_shape=(jax.ShapeDtypeStruct((b * t, d), BF16),
                   jax.ShapeDtypeStruct((b, SUBLANES, d), F32)),
        scratch_shapes=[pltpu.VMEM((tt + SUBLANES, d), F32), pltpu.VMEM((SUBLANES, d), F32)],
        compiler_params=_cparams(("parallel", "arbitrary")),
        name="rglru_prompt",
    )(xr, gr, cw, cb, wa, ba, wi, bi_, clam)
    return y, hl[:, 0, :]


def _rglru_sample_kernel(xp_ref, gr_ref, h0_ref, cw_ref, cb_ref, wa_ref, ba_ref, wi_ref, bi_ref,
                         clam_ref, y_ref, hs_ref):
    bt, t, d = gr_ref.shape
    xc = cb_ref[...].reshape(1, 1, d)
    for j in range(CONV_W):
        xc = xc + cw_ref[j:j + 1, :].reshape(1, 1, d) * xp_ref[:, j:j + t, :]
    xc2 = xc.reshape(bt * t, d)
    a, u = _lru_gates(xc2, wa_ref, ba_ref, wi_ref, bi_ref, clam_ref)
    a3, u3 = _scan8(a.reshape(bt, t, d), u.reshape(bt, t, d))
    h = a3 * h0_ref[...] + u3
    hs_ref[...] = h
    y_ref[...] = (h * _gelu_tanh(gr_ref[...])).astype(y_ref.dtype)


def _rglru_sample(xp, gr3, h0, lru_w, bt):
    nb, t, d = gr3.shape
    cw, cb, wa, ba, wi, bi_, clam = lru_w
    full2 = lambda i: (0, 0)
    full3 = lambda i: (0, 0, 0)
    blk = lambda i: (i, 0, 0)
    return pl.pallas_call(
        _rglru_sample_kernel,
        grid=(nb // bt,),
        in_specs=[pl.BlockSpec((bt, t + CONV_W - 1, d), blk), pl.BlockSpec((bt, t, d), blk),
                  pl.BlockSpec((bt, 1, d), blk),
                  pl.BlockSpec(cw.shape, full2), pl.BlockSpec(cb.shape, full2),
                  pl.BlockSpec(wa.shape, full3), pl.BlockSpec(ba.shape, full2),
                  pl.BlockSpec(wi.shape, full3), pl.BlockSpec(bi_.shape, full2),
                  pl.BlockSpec(clam.shape, full2)],
        out_specs=(pl.BlockSpec((bt, t, d), blk), pl.BlockSpec((bt, t, d), blk)),
        out_shape=(jax.ShapeDtypeStruct((nb, t, d), BF16), jax.ShapeDtypeStruct((nb, t, d), F32)),
        compiler_params=_cparams(("parallel",)),
        name="rglru_sample",
    )(xp, gr3, h0, cw, cb, wa, ba, wi, bi_, clam)


def _prep_lru(conv_w, conv_b, w_a, b_a, w_i, b_i, lru_lambda):
    d = conv_w.shape[1]
    per = 256 // w_a.shape[1]
    clam = (-LRU_C * jax.nn.softplus(-lru_lambda.astype(F32))).reshape(1, d)
    return (conv_w, conv_b.reshape(1, d), _block_diag(w_a, per).astype(BF16), b_a.reshape(1, d),
            _block_diag(w_i, per).astype(BF16), b_i.reshape(1, d), clam)


INT_MIN = -2 ** 31
INT_MAX = 2 ** 31 - 1
M_INIT = -1e30
L_FLOOR = 1e-30


def _expand_pairs(x, ones_lane=False):
    rows, w = x.shape
    lane = lax.broadcasted_iota(I32, (rows, LANES), 1)
    lo = lane < HEAD_DIM
    z_lo = jnp.where(lane == 0, 1.0, 0.0) if ones_lane else jnp.zeros((rows, LANES), F32)
    z_hi = jnp.where(lane == HEAD_DIM, 1.0, 0.0) if ones_lane else jnp.zeros((rows, LANES), F32)
    outs = []
    for j in range(w // LANES):
        blk = x[:, j * LANES:(j + 1) * LANES]
        rol = pltpu.roll(blk, HEAD_DIM, axis=1)
        outs += [jnp.where(lo, blk, z_hi), jnp.where(lo, z_lo, rol),
                 jnp.where(lo, rol, z_hi), jnp.where(lo, z_lo, blk)]
    return jnp.concatenate(outs, axis=1).astype(BF16)


F32_MANT_BITS = 23
F32_EXP_BITS = 8
MANT_MASK = (1 << F32_MANT_BITS) - 1
EXP_BIAS = 127


def _key_parts(key):
    neg = key < 0
    mag = jnp.where(neg, key ^ INT_MAX, key) & INT_MAX
    return neg, mag >> F32_MANT_BITS, mag & MANT_MASK


def _pow2_biased(e):
    top = jnp.where(e > EXP_BIAS, float(2.0 ** (1 << (F32_EXP_BITS - 2))), 1.0)
    p = top
    for b in range(F32_EXP_BITS - 2, -1, -1):
        p = p * jnp.where(((e >> b) & 1) == 1, 1.0, float(2.0 ** -(1 << b)))
    return p * top


def _mantissa_value(m):
    return (m + (1 << F32_MANT_BITS)).astype(F32) * 2.0 ** -F32_MANT_BITS


def _key_scale(key):
    neg, e, _ = _key_parts(key)
    mag = jnp.where(e == 0, 0.0, _pow2_biased(e))
    return jnp.where(neg, -mag, mag)


def _key_value(key):
    _, _, m = _key_parts(key)
    return _key_scale(key) * _mantissa_value(m)


def _sparse_attend(q, iq, iw, qpos, kexp, vexp, ikexp, nck, tk, n_keep, idx_bits, kmax2, o_ref,
                   key_ref, bias_ref, m_ref, l_ref, acc_ref, j_ref):
    tq = q.shape[0]
    kf = float(n_keep)
    lane_pos = lax.broadcasted_iota(I32, (tq, tk), 1)

    lhs_i = jnp.concatenate([iq[:, m * LANES:(m + 1) * LANES] for m in range(IDX_HEADS // 2)], axis=0)

    def p1(c, carry):
        start = pl.multiple_of(c * tk, tk)
        s_even = _dot_nt(lhs_i, ikexp[0, pl.ds(start, tk), :])
        s_odd = _dot_nt(lhs_i, ikexp[1, pl.ds(start, tk), :])
        sc = jnp.zeros((tq, tk), F32)
        for m in range(IDX_HEADS // 2):
            sc = sc + jnp.maximum(s_even[m * tq:(m + 1) * tq], 0.0) * iw[:, 2 * m:2 * m + 1]
            sc = sc + jnp.maximum(s_odd[m * tq:(m + 1) * tq], 0.0) * iw[:, 2 * m + 1:2 * m + 2]
        key_ref[c] = jnp.where(lane_pos + start <= qpos, sc, NEG_INF)
        return carry

    lax.fori_loop(0, nck, p1, 0)

    def lane_fold(w):
        part = w[:, 0:LANES]
        for j in range(1, tk // LANES):
            part = part + w[:, j * LANES:(j + 1) * LANES]
        return part

    def count(weight):
        def body(c, acc):
            return acc + lane_fold(weight(key_ref[c], lane_pos + c * tk))
        acc = lax.fori_loop(0, nck, body, jnp.zeros((tq, LANES), F32))
        return jnp.sum(acc, axis=1, keepdims=True)

    def count_ge(value):
        return count(lambda kk, kpos: jnp.where(kk >= value, 1.0, 0.0))

    thr_key = jnp.where(count_ge(0.0) >= kf, jnp.zeros((tq, 1), I32), jnp.full((tq, 1), INT_MIN, I32))

    def exponent_bit(it, key):
        cand = key + jnp.left_shift(jnp.int32(1), 30 - it)
        return jnp.where(count_ge(_key_value(cand)) >= kf, cand, key)

    thr_key = lax.fori_loop(0, F32_EXP_BITS, exponent_bit, thr_key)
    scale = _key_scale(thr_key)
    flip = jnp.where(thr_key < 0, MANT_MASK, 0)

    def mantissa_bit(it, key):
        cand = key + jnp.left_shift(jnp.int32(1), F32_MANT_BITS - 1 - it)
        value = scale * _mantissa_value((cand ^ flip) & MANT_MASK)
        return jnp.where(count_ge(value) >= kf, cand, key)

    thr_key = lax.fori_loop(0, F32_MANT_BITS, mantissa_bit, thr_key)
    thr = _key_value(thr_key)

    cnt_ge = count_ge(thr)
    cnt_gt = count(lambda kk, kpos: jnp.where(kk > thr, 1.0, 0.0))
    need = kf - cnt_gt
    j_ref[...] = jnp.full((tq, 1), INT_MAX, I32)

    @pl.when(jnp.max(cnt_ge) > kf)
    def _():
        def p3(it, jj):
            cand = jj + jnp.left_shift(jnp.int32(1), idx_bits - 1 - it)
            cnt = count(lambda kk, kpos: jnp.where(kk == thr, jnp.where(kpos < cand, 1.0, 0.0), 0.0))
            return jnp.where(cnt < need, cand, jj)
        jj = lax.fori_loop(0, idx_bits, p3, jnp.zeros((tq, 1), I32))
        j_ref[...] = jnp.where(cnt_ge > kf, jj, INT_MAX)

    jlast = j_ref[...]

    qscale = HEAD_DIM ** -0.5 * math.log2(math.e)
    lhs = []
    for g in range(N_KV_HEADS):
        blk = jnp.concatenate([q[:, (2 * g) * LANES:(2 * g + 1) * LANES],
                               q[:, (2 * g + 1) * LANES:(2 * g + 2) * LANES]], axis=0)
        lhs.append((blk.astype(F32) * qscale).astype(BF16))
    lo_half = lax.broadcasted_iota(I32, (2 * tq, LANES), 1) < HEAD_DIM

    qmax2 = jnp.zeros((tq, 1), F32)
    for g in range(N_KV_HEADS):
        qf = lhs[g].astype(F32)
        sq = qf * qf
        n2 = jnp.maximum(jnp.sum(jnp.where(lo_half, sq, 0.0), axis=1, keepdims=True),
                         jnp.sum(jnp.where(lo_half, 0.0, sq), axis=1, keepdims=True))
        qmax2 = jnp.maximum(qmax2, jnp.maximum(n2[0:tq], n2[tq:2 * tq]))
    neg_shift = -jnp.sqrt(qmax2 * kmax2)

    def p3b(c, carry):
        kk = key_ref[c]
        kpos = lane_pos + c * tk
        tie = jnp.where(kk == thr, jnp.where(kpos <= jlast, neg_shift, NEG_INF), NEG_INF)
        bias_ref[c] = jnp.where(kpos <= qpos, jnp.where(kk > thr, neg_shift, tie), NEG_INF)
        return carry

    lax.fori_loop(0, nck, p3b, 0)

    def stacks(c):
        start = pl.multiple_of(c * tk, tk)
        b1 = bias_ref[c]
        bias2 = jnp.concatenate([b1, b1], axis=0)
        for g in range(N_KV_HEADS):
            for par in range(2):
                idx = 2 * g + par
                ke = kexp[idx, pl.ds(start, tk), :]
                ve = vexp[idx, pl.ds(start, tk), :]
                yield idx, _dot_nt(lhs[g], ke) + bias2, ve

    def assemble(denoms):
        outs = []
        for g in range(N_KV_HEADS):
            l_lo, l_hi = denoms(g)
            o = jnp.where(lo_half, acc_ref[2 * g] * (1.0 / l_lo), acc_ref[2 * g + 1] * (1.0 / l_hi))
            outs.append(o[0:tq])
            outs.append(o[tq:2 * tq])
        o_ref[...] = jnp.concatenate(outs, axis=1).astype(o_ref.dtype)

    def attend_exact():
        m_ref[...] = jnp.full(m_ref.shape, M_INIT, F32)
        l_ref[...] = jnp.zeros(l_ref.shape, F32)
        acc_ref[...] = jnp.zeros(acc_ref.shape, F32)

        def body(c, carry):
            for idx, s, ve in stacks(c):
                m_old = m_ref[idx]
                m_new = jnp.maximum(m_old, jnp.max(s, axis=1, keepdims=True))
                alpha = jnp.exp2(m_old - m_new)
                p = jnp.exp2(s - m_new)
                l_ref[idx] = alpha * l_ref[idx] + jnp.sum(p, axis=1, keepdims=True)
                acc_ref[idx] = alpha * acc_ref[idx] + _dot(p.astype(BF16), ve)
                m_ref[idx] = m_new
            return carry

        lax.fori_loop(0, nck, body, 0)
        assemble(lambda g: (l_ref[2 * g], l_ref[2 * g + 1]))

    acc_ref[...] = jnp.zeros(acc_ref.shape, F32)

    def body(c, carry):
        for idx, s, ve in stacks(c):
            acc_ref[idx] = acc_ref[idx] + _dot(jnp.exp2(s).astype(BF16), ve)
        return carry

    lax.fori_loop(0, nck, body, 0)

    def denoms(g):
        return acc_ref[2 * g][:, HEAD_DIM:HEAD_DIM + 1], acc_ref[2 * g + 1][:, 0:1]

    lmin = jnp.full((2 * tq, 1), jnp.inf, F32)
    for g in range(N_KV_HEADS):
        l_lo, l_hi = denoms(g)
        lmin = jnp.minimum(lmin, jnp.minimum(l_lo, l_hi))
    healthy = jnp.min(lmin) >= L_FLOOR
    assemble(denoms)

    @pl.when(jnp.logical_not(healthy))
    def _():
        attend_exact()


def _attn_scratch(tq, tk, nck_max):
    n_stack = 2 * N_KV_HEADS
    return [pltpu.VMEM((nck_max, tq, tk), F32), pltpu.VMEM((nck_max, tq, tk), F32),
            pltpu.VMEM((n_stack, 2 * tq, 1), F32), pltpu.VMEM((n_stack, 2 * tq, 1), F32),
            pltpu.VMEM((n_stack, 2 * tq, LANES), F32), pltpu.VMEM((tq, 1), I32)]


def _attn_prompt_kernel(q_ref, iq_ref, iw_ref, k_ref, v_ref, ik_ref, o_ref,
                        kexp, vexp, ikexp, kmax_ref, key_ref, bias_ref, m_ref, l_ref, acc_ref, j_ref,
                        *, tk, n_keep, idx_bits):
    i = pl.program_id(1)
    tq = q_ref.shape[0]
    t = k_ref.shape[0]

    @pl.when(i == 0)
    def _():
        kmax_ref[0] = jnp.float32(0.0)

        def fill(c, carry):
            rows = pl.ds(pl.multiple_of(c * tk, tk), tk)
            ke = _expand_pairs(k_ref[rows, :])
            ve = _expand_pairs(v_ref[rows, :], ones_lane=True)
            for s in range(2 * N_KV_HEADS):
                kexp[s, rows, :] = ke[:, s * LANES:(s + 1) * LANES]
                vexp[s, rows, :] = ve[:, s * LANES:(s + 1) * LANES]
            n2 = jnp.zeros((tk, 1), F32)
            for g in range(N_KV_HEADS):
                kf = ke[:, 2 * g * LANES:(2 * g + 1) * LANES].astype(F32)
                n2 = jnp.maximum(n2, jnp.sum(kf * kf, axis=1, keepdims=True))
            kmax_ref[0] = jnp.maximum(kmax_ref[0], jnp.max(n2))
            ik = ik_ref[rows, :].astype(F32)
            ikexp[0, rows, :] = ik.astype(BF16)
            ikexp[1, rows, :] = pltpu.roll(ik, HEAD_DIM, axis=1).astype(BF16)
            return carry
        lax.fori_loop(0, t // tk, fill, 0)

    qpos = i * tq + lax.broadcasted_iota(I32, (tq, 1), 0)
    nck = lax.div((i + 1) * tq - 1, tk) + 1
    _sparse_attend(q_ref[...], iq_ref[...], iw_ref[...], qpos, kexp, vexp, ikexp, nck, tk,
                   n_keep, idx_bits, kmax_ref[0], o_ref, key_ref, bias_ref, m_ref, l_ref, acc_ref, j_ref)


def _attn_prompt(q, iq, iw, k, v, ikb, b, t, tq, tk):
    nq = t // tq
    n_keep = min(TOPK_KEYS, t // 4)
    row = lambda bi, i: (bi * nq + i, 0)
    per_b = lambda bi, i: (bi, 0)
    dq, dk = q.shape[1], k.shape[1]
    kern = functools.partial(_attn_prompt_kernel, tk=tk, n_keep=n_keep,
                             idx_bits=max(1, math.ceil(math.log2(t))))
    return pl.pallas_call(
        kern,
        grid=(b, nq),
        in_specs=[pl.BlockSpec((tq, dq), row), pl.BlockSpec((tq, iq.shape[1]), row),
                  pl.BlockSpec((tq, iw.shape[1]), row),
                  pl.BlockSpec((t, dk), per_b, pipeline_mode=pl.Buffered(1)),
                  pl.BlockSpec((t, dk), per_b, pipeline_mode=pl.Buffered(1)),
                  pl.BlockSpec((t, LANES), per_b, pipeline_mode=pl.Buffered(1))],
        out_specs=pl.BlockSpec((tq, dq), row),
        out_shape=jax.ShapeDtypeStruct((b * t, dq), BF16),
        scratch_shapes=[pltpu.VMEM((2 * N_KV_HEADS, t, LANES), BF16), pltpu.VMEM((2 * N_KV_HEADS, t, LANES), BF16),
                        pltpu.VMEM((2, t, LANES), BF16), pltpu.SMEM((1,), F32)]
        + _attn_scratch(tq, tk, t // tk),
        compiler_params=_cparams(("parallel", "arbitrary")),
        name="attn_prompt",
    )(q, iq, iw, k, v, ikb)


SAMPLE_Q_ROWS = 16


def _entry_rows(src_ref, f32_scr, t_new):
    f32_scr[...] = src_ref[...].astype(F32)
    per_blk = SAMPLE_Q_ROWS // t_new
    off = pl.multiple_of(lax.rem(pl.program_id(0), per_blk) * t_new, t_new)
    return f32_scr[pl.ds(off, t_new), :]


def _sample_geometry(n_pages, page, t_new):
    assert N_HEADS * t_new == LANES and IDX_HEADS * t_new == IDX_DIM and t_new <= page
    past = n_pages * page
    return past, past + page, LANES // t_new


def _with_zero_tail(x, rows):
    return jnp.concatenate([x, jnp.zeros((rows - x.shape[0], x.shape[1]), x.dtype)], axis=0)


def _sample_scores_kernel(pt_ref, iq_ref, w_ref, ikn_ref, *rest, n_pages, t_new, group):
    ikpages = rest[:n_pages]
    o_ref, ikt, iq_scr, ikn_scr = rest[n_pages:]
    bl = lax.rem(pl.program_id(0), group)
    page = ikpages[0].shape[2]
    past = n_pages * page
    for p in range(n_pages):
        ikt[:, p * page:(p + 1) * page] = ikpages[p][0].astype(BF16)
    tail = _with_zero_tail(_entry_rows(ikn_ref, ikn_scr, t_new), page).T
    ikt[:, past:past + page] = tail[0:IDX_DIM].astype(BF16)

    iqf = _entry_rows(iq_ref, iq_scr, t_new)
    pieces = []
    for h in range(IDX_HEADS):
        blk = iqf[:, (h // 2) * LANES:(h // 2 + 1) * LANES]
        if h % 2:
            blk = pltpu.roll(blk, IDX_DIM, axis=1)
        pieces.append(blk[:, :IDX_DIM])
    iqm = jnp.concatenate(pieces, axis=0).astype(BF16)
    r = jnp.maximum(_dot(iqm, ikt[...]), 0.0) * w_ref[0]
    sc = r[0:t_new]
    for h in range(1, IDX_HEADS):
        sc = sc + r[h * t_new:(h + 1) * t_new]
    o_ref[0, pl.ds(pl.multiple_of(bl * t_new, t_new), t_new), :] = sc


def _sample_select_kernel(sc_ref, sel_ref, key_ref, j_ref, *, past, t_new, n_keep, idx_bits):
    rows, s_pad = key_ref.shape
    kpos = lax.broadcasted_iota(I32, (rows, s_pad), 1)
    valid = kpos <= past + lax.rem(lax.broadcasted_iota(I32, (rows, s_pad), 0), t_new)
    key_ref[...] = jnp.where(valid, sc_ref[0], NEG_INF)
    kf = float(n_keep)

    def count(weight):
        return jnp.sum(weight(key_ref[...]), axis=1, keepdims=True)

    def count_ge(value):
        return count(lambda kk: jnp.where(kk >= value, 1.0, 0.0))

    thr_key = jnp.where(count_ge(0.0) >= kf, jnp.zeros((rows, 1), I32), jnp.full((rows, 1), INT_MIN, I32))

    def key_bit(it, key):
        cand = key + jnp.left_shift(jnp.int32(1), 30 - it)
        return jnp.where(count_ge(_key_value(cand)) >= kf, cand, key)

    thr = _key_value(lax.fori_loop(0, F32_EXP_BITS + F32_MANT_BITS, key_bit, thr_key))
    cnt_ge = count_ge(thr)
    need = kf - count(lambda kk: jnp.where(kk > thr, 1.0, 0.0))
    j_ref[...] = jnp.full(j_ref.shape, INT_MAX, I32)

    @pl.when(jnp.max(cnt_ge) > kf)
    def _():
        def p3(it, jj):
            cand = jj + jnp.left_shift(jnp.int32(1), idx_bits - 1 - it)
            cnt = count(lambda kk: jnp.where(kk == thr, jnp.where(kpos < cand, 1.0, 0.0), 0.0))
            return jnp.where(cnt < need, cand, jj)
        jj = lax.fori_loop(0, idx_bits, p3, jnp.zeros((rows, 1), I32))
        j_ref[...] = jnp.where(cnt_ge > kf, jj, INT_MAX)

    jlast = j_ref[...]
    kk = key_ref[...]
    tie = jnp.where(kk == thr, jnp.where(kpos <= jlast, 1.0, 0.0), 0.0)
    sel_ref[0] = jnp.where(valid, jnp.where(kk > thr, 1.0, tie), 0.0)


def _sample_attend_kernel(pt_ref, q_ref, kn_ref, vn_ref, sel_ref, *rest, n_pages, t_new, group):
    kpages = rest[:n_pages]
    vpages = rest[n_pages:2 * n_pages]
    o_ref, kt, vt, q_scr = rest[2 * n_pages:]
    bl = lax.rem(pl.program_id(0), group)
    page = kpages[0].shape[2]
    past = n_pages * page
    dk, s_pad = kt.shape
    for p in range(n_pages):
        cols = slice(p * page, (p + 1) * page)
        kt[:, cols] = kpages[p][0].astype(BF16)
        vt[:, cols] = vpages[p][0].astype(BF16)
    kt[:, past:s_pad] = _with_zero_tail(kn_ref[...], page).T.astype(BF16)
    vt[:, past:s_pad] = _with_zero_tail(vn_ref[...], page).T.astype(BF16)

    qf = _entry_rows(q_ref, q_scr, t_new) * (HEAD_DIM ** -0.5 * math.log2(math.e))
    lo = lax.broadcasted_iota(I32, (t_new, LANES), 1) < HEAD_DIM
    zero = jnp.zeros((t_new, LANES), F32)
    per_group = N_HEADS // N_KV_HEADS
    qrows = []
    for h in range(N_HEADS):
        g = h // per_group
        blk = qf[:, (h // 2) * LANES:(h // 2 + 1) * LANES]
        if h % 2 != g % 2:
            blk = pltpu.roll(blk, HEAD_DIM, axis=1)
        blk = jnp.where(lo, blk, zero) if g % 2 == 0 else jnp.where(lo, zero, blk)
        qrows.append(jnp.concatenate([blk, zero] if g // 2 == 0 else [zero, blk], axis=1))
    qbd = jnp.concatenate(qrows, axis=0).astype(BF16)
    s = _dot(qbd, kt[...])

    picked = sel_ref[0, pl.ds(pl.multiple_of(bl * t_new, t_new), t_new), :]
    s = jnp.where(jnp.concatenate([picked] * N_HEADS, axis=0) > 0.5, s, NEG_INF)
    p = jnp.exp2(s - jnp.max(s, axis=1, keepdims=True))
    o = _dot_nt(p.astype(BF16), vt[...]) / jnp.sum(p, axis=1, keepdims=True)

    outs = []
    for m in range(N_HEADS // 2):
        pair = []
        for h in (2 * m, 2 * m + 1):
            g = h // per_group
            piece = o[h * t_new:(h + 1) * t_new, (g // 2) * LANES:(g // 2 + 1) * LANES]
            pair.append(pltpu.roll(piece, HEAD_DIM, axis=1) if g % 2 != h % 2 else piece)
        outs.append(jnp.where(lo, pair[0], pair[1]))
    o_ref[...] = jnp.concatenate(outs, axis=1)


def _attn_sample(q, iq, iw_col, k_new, v_new, ik_new, cache_kt, cache_vt, cache_ikt, page_table, t_new):
    nb, n_pages = page_table.shape
    n_pool, dk, page = cache_kt.shape
    past, s_pad, group = _sample_geometry(n_pages, page, t_new)
    per_blk = SAMPLE_Q_ROWS // t_new
    assert nb % group == 0 and nb % per_blk == 0 and t_new % SUBLANES == 0
    n_groups = nb // group
    n_keep = min(TOPK_KEYS, (past + t_new) // 4)
    dq = q.shape[1]
    rows = group * t_new
    row = lambda b, pt: (b, 0)
    qrow = lambda b, pt: (b // per_blk, 0)
    grp = lambda b, pt: (b // group, 0, 0)

    def page_spec(height, p):
        return pl.BlockSpec((1, height, page), lambda b, pt, p=p: (pt[b, p], 0, 0))

    scores = pl.pallas_call(
        functools.partial(_sample_scores_kernel, n_pages=n_pages, t_new=t_new, group=group),
        grid_spec=pltpu.PrefetchScalarGridSpec(
            num_scalar_prefetch=1,
            grid=(nb,),
            in_specs=[pl.BlockSpec((SAMPLE_Q_ROWS, iq.shape[1]), qrow),
                      pl.BlockSpec((1, iw_col.shape[1], 1), lambda b, pt: (b, 0, 0)),
                      pl.BlockSpec((SAMPLE_Q_ROWS, ik_new.shape[1]), qrow)]
            + [page_spec(IDX_DIM, p) for p in range(n_pages)],
            out_specs=pl.BlockSpec((1, rows, s_pad), grp),
            scratch_shapes=[pltpu.VMEM((IDX_DIM, s_pad), BF16), pltpu.VMEM((SAMPLE_Q_ROWS, iq.shape[1]), F32),
                            pltpu.VMEM((SAMPLE_Q_ROWS, ik_new.shape[1]), F32)]),
        out_shape=jax.ShapeDtypeStruct((n_groups, rows, s_pad), F32),
        compiler_params=_cparams(("arbitrary",)),
        name="sample_scores",
    )(page_table, iq, iw_col, ik_new, *([cache_ikt] * n_pages))

    sel = pl.pallas_call(
        functools.partial(_sample_select_kernel, past=past, t_new=t_new, n_keep=n_keep,
                          idx_bits=max(1, math.ceil(math.log2(s_pad)))),
        grid=(n_groups,),
        in_specs=[pl.BlockSpec((1, rows, s_pad), lambda g: (g, 0, 0))],
        out_specs=pl.BlockSpec((1, rows, s_pad), lambda g: (g, 0, 0)),
        out_shape=jax.ShapeDtypeStruct((n_groups, rows, s_pad), F32),
        scratch_shapes=[pltpu.VMEM((rows, s_pad), F32), pltpu.VMEM((rows, 1), I32)],
        compiler_params=_cparams(("parallel",)),
        name="sample_select",
    )(scores)

    return pl.pallas_call(
        functools.partial(_sample_attend_kernel, n_pages=n_pages, t_new=t_new, group=group),
        grid_spec=pltpu.PrefetchScalarGridSpec(
            num_scalar_prefetch=1,
            grid=(nb,),
            in_specs=[pl.BlockSpec((SAMPLE_Q_ROWS, dq), qrow),
                      pl.BlockSpec((t_new, dk), row), pl.BlockSpec((t_new, dk), row),
                      pl.BlockSpec((1, rows, s_pad), grp)]
            + [page_spec(dk, p) for p in range(n_pages)] + [page_spec(dk, p) for p in range(n_pages)],
            out_specs=pl.BlockSpec((t_new, dq), row),
            scratch_shapes=[pltpu.VMEM((dk, s_pad), BF16), pltpu.VMEM((dk, s_pad), BF16),
                            pltpu.VMEM((SAMPLE_Q_ROWS, dq), F32)]),
        out_shape=jax.ShapeDtypeStruct((nb * t_new, dq), F32),
        compiler_params=_cparams(("parallel",)),
        name="sample_attend",
    )(page_table, q, k_new, v_new, sel, *([cache_kt] * n_pages), *([cache_vt] * n_pages))


def _layer_norm(x, g, b):
    mu = jnp.mean(x, axis=-1, keepdims=True)
    xc = x - mu
    var = jnp.mean(xc * xc, axis=-1, keepdims=True)
    return xc * lax.rsqrt(var + LN_EPS) * g + b


def _merge_kernel(*refs, alpha, first_tiles):
    wa_ref, wb_ref, wo_ref, g_ref, b_ref, h_ref = refs[10:]

    def run(yr_ref, ya_ref, ga_ref, gb_ref, x_ref):
        pa = _dot(yr_ref[...].astype(BF16), wa_ref[...])
        pb = _dot(ya_ref[...].astype(BF16), wb_ref[...])
        m = jax.nn.sigmoid(ga_ref[...]) * pa + jax.nn.sigmoid(gb_ref[...]) * pb
        mix = _dot(m.astype(BF16), wo_ref[...])
        h_ref[...] = _layer_norm(alpha * x_ref[...] + mix, g_ref[...], b_ref[...])

    @pl.when(pl.program_id(0) < first_tiles)
    def _():
        run(*refs[0:5])

    @pl.when(pl.program_id(0) >= first_tiles)
    def _():
        run(*refs[5:10])


def _merge(first, second, wbr_a, wbr_b, w_out, ln_g, ln_b, alpha, tm):
    n1, d = first[4].shape
    n2 = second[4].shape[0]
    t1, t2 = n1 // tm, n2 // tm
    in_first = lambda i: (jnp.minimum(i, t1 - 1), 0)
    in_second = lambda i: (jnp.maximum(i - t1, 0), 0)
    full = lambda i: (0, 0)
    return pl.pallas_call(
        functools.partial(_merge_kernel, alpha=alpha, first_tiles=t1),
        grid=(t1 + t2,),
        in_specs=[pl.BlockSpec((tm, d), in_first)] * 5 + [pl.BlockSpec((tm, d), in_second)] * 5
        + [pl.BlockSpec((d, d), full)] * 3 + [pl.BlockSpec((1, d), full)] * 2,
        out_specs=pl.BlockSpec((tm, d), lambda i: (i, 0)),
        out_shape=jax.ShapeDtypeStruct((n1 + n2, d), F32),
        compiler_params=_cparams(("arbitrary",)),
        name="merge_ln1",
    )(*first, *second, wbr_a, wbr_b, w_out, ln_g, ln_b)


def _router_kernel(h_ref, whi_ref, wlo_ref, bias_ref, e_ref, g_ref, r_ref, cnt_ref, run_ref):
    h = h_ref[...]
    tm = h.shape[0]
    ne = whi_ref.shape[1]
    h_hi = h.astype(BF16)
    h_lo = (h - h_hi.astype(F32)).astype(BF16)
    logits = _dot(h_hi, whi_ref[...]) + (_dot(h_lo, whi_ref[...]) + _dot(h_hi, wlo_ref[...]))
    scores = jax.nn.sigmoid(logits)
    biased = scores + bias_ref[...]
    lane = lax.broadcasted_iota(I32, (tm, ne), 1)
    lane_f = lane.astype(F32)
    per_group = ne // N_EXPERT_GROUPS
    big = float(ne)

    def first_argmax(v):
        m = jnp.max(v, axis=1, keepdims=True)
        idx = jnp.min(jnp.where(v == m, lane_f, big), axis=1, keepdims=True)
        return m, idx

    gscore = []
    for g in range(N_EXPERT_GROUPS):
        in_g = (lane >= g * per_group) & (lane < (g + 1) * per_group)
        mg = jnp.where(in_g, biased, NEG_INF)
        m1, i1 = first_argmax(mg)
        m2 = jnp.max(jnp.where(lane_f == i1, NEG_INF, mg), axis=1, keepdims=True)
        gscore.append(m1 + m2)

    ok_map = jnp.zeros((tm, ne), F32)
    for g in range(N_EXPERT_GROUPS):
        rank = jnp.zeros((tm, 1), F32)
        for o in range(N_EXPERT_GROUPS):
            if o == g:
                continue
            ahead = (gscore[o] > gscore[g]) if o > g else (gscore[o] >= gscore[g])
            rank = rank + jnp.where(ahead, 1.0, 0.0)
        in_g = (lane >= g * per_group) & (lane < (g + 1) * per_group)
        ok_map = jnp.where(in_g, jnp.where(rank < float(TOPK_GROUPS), 1.0, 0.0), ok_map)

    cur = jnp.where(ok_map > 0.5, biased, NEG_INF)
    out_lane = lax.broadcasted_iota(I32, (tm, LANES), 1)
    e_out = jnp.zeros((tm, LANES), F32)
    s_out = jnp.zeros((tm, LANES), F32)
    total = jnp.zeros((tm, 1), F32)
    picked = jnp.zeros((tm, ne), F32)
    hits = []
    for j in range(EXPERT_TOP_K):
        _, idx = first_argmax(cur)
        hit = lane_f == idx
        hits.append(hit)
        sel = jnp.sum(jnp.where(hit, scores, 0.0), axis=1, keepdims=True)
        cur = jnp.where(hit, NEG_INF, cur)
        picked = jnp.where(hit, 1.0, picked)
        e_out = jnp.where(out_lane == j, idx, e_out)
        s_out = jnp.where(out_lane == j, sel, s_out)
        total = total + sel
    e_ref[...] = e_out
    g_ref[...] = ROUTED_SCALE * s_out / total

    @pl.when(pl.program_id(0) == 0)
    def _():
        run_ref[...] = jnp.zeros_like(run_ref)

    pk = picked.astype(BF16)
    r_i = lax.broadcasted_iota(I32, (tm, tm), 0)
    c_i = lax.broadcasted_iota(I32, (tm, tm), 1)
    before = _dot(jnp.where(c_i < r_i, 1.0, 0.0).astype(BF16), pk) + run_ref[0:1, :]
    r_out = jnp.zeros((tm, LANES), F32)
    for j in range(EXPERT_TOP_K):
        rank = jnp.sum(jnp.where(hits[j], before, 0.0), axis=1, keepdims=True)
        r_out = jnp.where(out_lane == j, rank, r_out)
    r_ref[...] = r_out
    run_ref[...] = run_ref[...] + _dot(jnp.ones((SUBLANES, tm), BF16), pk)
    cnt_ref[...] = run_ref[...]


def _router(h, w_hi, w_lo, bias, tm):
    n, d = h.shape
    ne = w_hi.shape[1]
    row = lambda i: (i, 0)
    full = lambda i: (0, 0)
    return pl.pallas_call(
        _router_kernel,
        grid=(n // tm,),
        in_specs=[pl.BlockSpec((tm, d), row), pl.BlockSpec((d, ne), full), pl.BlockSpec((d, ne), full),
                  pl.BlockSpec((1, ne), full)],
        out_specs=(pl.BlockSpec((tm, LANES), row), pl.BlockSpec((tm, LANES), row),
                   pl.BlockSpec((tm, LANES), row), pl.BlockSpec((SUBLANES, ne), full)),
        out_shape=(jax.ShapeDtypeStruct((n, LANES), F32), jax.ShapeDtypeStruct((n, LANES), F32),
                   jax.ShapeDtypeStruct((n, LANES), F32), jax.ShapeDtypeStruct((SUBLANES, ne), F32)),
        scratch_shapes=[pltpu.VMEM((SUBLANES, ne), F32)],
        compiler_params=_cparams(("arbitrary",)),
        name="router",
    )(h, w_hi, w_lo, bias)


def _row_layout(counts, n_rows):
    n_experts = counts.shape[0]
    n_blocks = n_rows // MOE_BLOCK
    padded = (counts + MOE_BLOCK - 1) // MOE_BLOCK * MOE_BLOCK
    pad_end = jnp.cumsum(padded)
    pad_start = pad_end - padded
    block_start = jnp.arange(n_blocks, dtype=I32) * MOE_BLOCK
    block_expert = jnp.minimum(jnp.sum((pad_end[None, :] <= block_start[:, None]).astype(I32), axis=1),
                               n_experts - 1).astype(I32)
    n_active = (pad_end[-1] // MOE_BLOCK).astype(I32).reshape(1)
    ids = jnp.arange(n_experts, dtype=I32)
    live = counts > 0
    later = jnp.flip(lax.cummin(jnp.flip(jnp.where(live, ids, n_experts))))
    nxt = jnp.concatenate([later[1:], jnp.full((1,), n_experts, I32)])
    next_expert = jnp.where(nxt < n_experts, nxt, -1).astype(I32)
    slot_of = ((jnp.cumsum(live.astype(I32)) - 1) & 1).astype(I32)
    return pad_start.astype(I32), pad_end.astype(I32), block_expert, n_active, next_expert, slot_of


def _dest_kernel(e_ref, r_ref, ps_ref, d_ref):
    e = e_ref[...]
    tm = e.shape[0]
    ne = ps_ref.shape[1]
    lane_f = lax.broadcasted_iota(I32, (tm, ne), 1).astype(F32)
    out_lane = lax.broadcasted_iota(I32, (tm, LANES), 1)
    ps = ps_ref[...]
    out = r_ref[...]
    for j in range(EXPERT_TOP_K):
        start = jnp.sum(jnp.where(lane_f == e[:, j:j + 1], ps, 0.0), axis=1, keepdims=True)
        out = jnp.where(out_lane == j, out + start, out)
    d_ref[...] = out.astype(I32)


def _dest(e_f, r_f, pad_start_f, tm):
    n = e_f.shape[0]
    ne = pad_start_f.shape[1]
    row = lambda i: (i, 0)
    return pl.pallas_call(
        _dest_kernel,
        grid=(n // tm,),
        in_specs=[pl.BlockSpec((tm, LANES), row), pl.BlockSpec((tm, LANES), row),
                  pl.BlockSpec((1, ne), lambda i: (0, 0))],
        out_specs=pl.BlockSpec((tm, LANES), row),
        out_shape=jax.ShapeDtypeStruct((n, LANES), I32),
        compiler_params=_cparams(("parallel",)),
        name="moe_dest",
    )(e_f, r_f, pad_start_f)


def _pow2_below(n):
    return [1 << b for b in range(n.bit_length() - 1, -1, -1)]


def _scatter_kernel(cnt_ref, ps_ref, pe_ref, dest_ref, h_ref, xs_hbm, zbuf, sem, zsem, *, top_k, n_rows):
    i = pl.program_id(0)
    tm = h_ref.shape[0]
    n_experts = cnt_ref.shape[0]

    def row_copy(t, j):
        return pltpu.make_async_copy(h_ref.at[pl.ds(t, 1), :],
                                     xs_hbm.at[pl.ds(dest_ref[0, 0, t * top_k + j], 1), :], sem)

    def issue(t, carry):
        for j in range(top_k):
            row_copy(t, j).start(priority=j % 2)
        return carry

    lax.fori_loop(0, tm, issue, 0)

    def zero_copy(start, size):
        return pltpu.make_async_copy(zbuf.at[pl.ds(0, size), :], xs_hbm.at[pl.ds(start, size), :], zsem)

    @pl.when(i == 0)
    def _():
        zbuf[...] = jnp.zeros_like(zbuf)

        def fill(first, end, wait):
            aligned = (first + (SUBLANES - 1)) & ~(SUBLANES - 1)
            for s in range(SUBLANES - 1):
                @pl.when(first + s < aligned)
                def _():
                    cp = zero_copy(first + s, 1)
                    cp.wait() if wait else cp.start()
            n_tiles = lax.div(end - aligned, jnp.int32(SUBLANES))
            for size in _pow2_below(MOE_BLOCK // SUBLANES - 1):
                @pl.when((n_tiles & size) != 0)
                def _():
                    start = aligned + (n_tiles & ~(2 * size - 1)) * SUBLANES
                    cp = zero_copy(pl.multiple_of(start, SUBLANES), size * SUBLANES)
                    cp.wait() if wait else cp.start()

        def per_expert(wait):
            def body(e, carry):
                fill(ps_ref[e] + cnt_ref[e], pe_ref[e], wait)
                return carry
            lax.fori_loop(0, n_experts, body, 0)

        def tail(wait):
            total = pe_ref[n_experts - 1]

            def body(b, carry):
                cp = zero_copy(pl.multiple_of(total + b * MOE_BLOCK, MOE_BLOCK), MOE_BLOCK)
                cp.wait() if wait else cp.start()
                return carry
            lax.fori_loop(0, lax.div(n_rows - total, jnp.int32(MOE_BLOCK)), body, 0)

        per_expert(False)
        tail(False)
        per_expert(True)
        tail(True)

    pltpu.make_async_copy(h_ref, xs_hbm.at[pl.ds(0, tm), :], sem).wait()
    for _ in range(top_k - 1):
        pltpu.make_async_copy(h_ref, xs_hbm.at[pl.ds(0, tm), :], sem).wait()


def _scatter_rows(h, dest, counts, pad_start, pad_end, n_rows, top_k, tm):
    n, d = h.shape
    grid_spec = pltpu.PrefetchScalarGridSpec(
        num_scalar_prefetch=3,
        grid=(n // tm,),
        in_specs=[pl.BlockSpec((1, 1, tm * top_k), lambda i, *_: (i, 0, 0), memory_space=pltpu.SMEM),
                  pl.BlockSpec((tm, d), lambda i, *_: (i, 0))],
        out_specs=pl.BlockSpec(memory_space=pl.ANY),
        scratch_shapes=[pltpu.VMEM((MOE_BLOCK, d), F32), pltpu.SemaphoreType.DMA(()),
                        pltpu.SemaphoreType.DMA(())])
    return pl.pallas_call(
        functools.partial(_scatter_kernel, top_k=top_k, n_rows=n_rows),
        grid_spec=grid_spec,
        out_shape=jax.ShapeDtypeStruct((n_rows, d), F32),
        compiler_params=_cparams(("arbitrary",)),
        name="moe_scatter",
    )(counts, pad_start, pad_end, dest.reshape(n // tm, 1, tm * top_k), h)


def _experts_kernel(be_ref, na_ref, nxt_ref, slot_ref, x_ref, wg_hbm, wu_hbm, wd_hbm, o_ref,
                    wg_f, wu_f, wd_f, wg_b, wu_b, wd_b, sem):
    i = pl.program_id(0)

    def fetch(e, slot):
        return (pltpu.make_async_copy(wg_hbm.at[e], wg_f.at[slot], sem.at[slot, 0]),
                pltpu.make_async_copy(wu_hbm.at[e], wu_f.at[slot], sem.at[slot, 1]),
                pltpu.make_async_copy(wd_hbm.at[e], wd_f.at[slot], sem.at[slot, 2]))

    @pl.when(i < na_ref[0])
    def _():
        e = be_ref[i]
        slot = slot_ref[e]

        @pl.when(i == 0)
        def _():
            for cp in fetch(e, slot):
                cp.start()

        @pl.when((i == 0) | (be_ref[jnp.maximum(i - 1, 0)] != e))
        def _():
            for cp in fetch(e, slot):
                cp.wait()
            nxt = nxt_ref[e]

            @pl.when(nxt >= 0)
            def _():
                for cp in fetch(nxt, 1 - slot):
                    cp.start()

            wg_b[...] = wg_f[slot].astype(BF16)
            wu_b[...] = wu_f[slot].astype(BF16)
            wd_b[...] = wd_f[slot].astype(BF16)

        xb = x_ref[...].astype(BF16)
        g = _dot(xb, wg_b[...])
        u = _dot(xb, wu_b[...])
        hb = (g * jax.nn.sigmoid(g)) * u
        o_ref[...] = _dot(hb.astype(BF16), wd_b[...])

    @pl.when(i >= na_ref[0])
    def _():
        o_ref[...] = jnp.zeros(o_ref.shape, o_ref.dtype)


def _experts(xs, block_expert, n_active, next_expert, slot_of, w_gate, w_up, w_down):
    n_rows, d = xs.shape
    ne, _, de = w_gate.shape
    n_blocks = n_rows // MOE_BLOCK
    live = lambda i, be, na, nx, sl: (jnp.minimum(i, na[0] - 1), 0)
    grid_spec = pltpu.PrefetchScalarGridSpec(
        num_scalar_prefetch=4,
        grid=(n_blocks,),
        in_specs=[pl.BlockSpec((MOE_BLOCK, d), live)] + [pl.BlockSpec(memory_space=pl.ANY)] * 3,
        out_specs=pl.BlockSpec((MOE_BLOCK, d), lambda i, be, na, nx, sl: (i, 0)),
        scratch_shapes=[pltpu.VMEM((2, d, de), F32), pltpu.VMEM((2, d, de), F32), pltpu.VMEM((2, de, d), F32),
                        pltpu.VMEM((d, de), BF16), pltpu.VMEM((d, de), BF16), pltpu.VMEM((de, d), BF16),
                        pltpu.SemaphoreType.DMA((2, 3))])
    return pl.pallas_call(
        _experts_kernel,
        grid_spec=grid_spec,
        out_shape=jax.ShapeDtypeStruct((n_rows, d), F32),
        compiler_params=_cparams(("arbitrary",)),
        name="moe_experts",
    )(block_expert, n_active, next_expert, slot_of, xs, w_gate, w_up, w_down)


def _combine_kernel(dest_ref, rows_hbm, h_ref, gate_ref, wsg_ref, wsu_ref, wsd_ref, g_ref, b_ref, y_ref,
                    gbuf, sem, *, alpha, top_k):
    tm = h_ref.shape[0]

    def issue(t, carry):
        for j in range(top_k):
            pltpu.make_async_copy(rows_hbm.at[pl.ds(dest_ref[0, 0, t * top_k + j], 1), :],
                                  gbuf.at[j, pl.ds(t, 1), :], sem).start(priority=j % 2)
        return carry

    lax.fori_loop(0, tm, issue, 0)
    h = h_ref[...]
    hb = h.astype(BF16)
    sg = _dot(hb, wsg_ref[...])
    su = _dot(hb, wsu_ref[...])
    shared = _dot(((sg * jax.nn.sigmoid(sg)) * su).astype(BF16), wsd_ref[...])
    gates = gate_ref[...]
    for j in range(top_k):
        pltpu.make_async_copy(rows_hbm.at[pl.ds(0, tm), :], gbuf.at[j], sem).wait()
    routed = gbuf[0] * gates[:, 0:1]
    for j in range(1, top_k):
        routed = routed + gbuf[j] * gates[:, j:j + 1]
    y_ref[...] = _layer_norm(alpha * h + (routed + shared), g_ref[...], b_ref[...])


def _combine(h, rows, dest, gates, wsg, wsu, wsd, ln_g, ln_b, alpha, top_k, tm):
    n, d = h.shape
    ds_ = wsg.shape[1]
    row = lambda i: (i, 0)
    full = lambda i: (0, 0)
    return pl.pallas_call(
        functools.partial(_combine_kernel, alpha=alpha, top_k=top_k),
        grid=(n // tm,),
        in_specs=[pl.BlockSpec((1, 1, tm * top_k), lambda i: (i, 0, 0), memory_space=pltpu.SMEM),
                  pl.BlockSpec(memory_space=pl.ANY),
                  pl.BlockSpec((tm, d), row), pl.BlockSpec((tm, LANES), row),
                  pl.BlockSpec((d, ds_), full), pl.BlockSpec((d, ds_), full), pl.BlockSpec((ds_, d), full),
                  pl.BlockSpec((1, d), full), pl.BlockSpec((1, d), full)],
        out_specs=pl.BlockSpec((tm, d), row),
        out_shape=jax.ShapeDtypeStruct((n, d), F32),
        scratch_shapes=[pltpu.VMEM((top_k, tm, d), F32), pltpu.SemaphoreType.DMA(())],
        compiler_params=_cparams(("arbitrary",)),
        name="moe_combine",
    )(dest.reshape(n // tm, 1, tm * top_k), rows, h, gates, wsg, wsu, wsd, ln_g, ln_b)


def _moe(h, w_r_hi, w_r_lo, r_bias, w_gate, w_up, w_down, wsg, wsu, wsd, ln_g, ln_b, alpha):
    n, d = h.shape
    ne = w_r_hi.shape[1]
    top_k = EXPERT_TOP_K
    n_rows = -(-(n * top_k + ne * (MOE_BLOCK - 1)) // MOE_BLOCK) * MOE_BLOCK
    e_f, g_f, r_f, cnt = _router(h, w_r_hi, w_r_lo, r_bias, 256)
    counts = cnt[0].astype(I32)
    pad_start, pad_end, block_expert, n_active, next_expert, slot_of = _row_layout(counts, n_rows)
    dest = _dest(e_f, r_f, pad_start.astype(F32).reshape(1, ne), 256)[:, :top_k]
    xs = _scatter_rows(h, dest, counts, pad_start, pad_end, n_rows, top_k, 256)
    ys = _experts(xs, block_expert, n_active, next_expert, slot_of, w_gate, w_up, w_down)
    return _combine(h, ys, dest, g_f, wsg, wsu, wsd, ln_g, ln_b, alpha, top_k, 256)


def _layer(xp, xs, cache_k, cache_v, cache_idx_k, state_conv, state_h, page_table, depth,
           w_in, conv_w, conv_b, w_a, b_a, w_i, b_i, lru_lambda, w_branch, w_out, ln1_g, ln1_b,
           w_router, router_bias, w_exp_gate, w_exp_up, w_exp_down, w_sh_gate, w_sh_up, w_sh_down,
           ln2_g, ln2_b):
    bp, tp, d = xp.shape
    bs, ts, _ = xs.shape
    n_pages = page_table.shape[1]
    past = n_pages * PAGE_SIZE
    alpha = (2.0 * depth) ** 0.25
    d_rnn = conv_w.shape[1]
    dkv = N_KV_HEADS * HEAD_DIM

    w_packed = _pack_w_in(w_in)
    lru_w = _prep_lru(conv_w, conv_b, w_a, b_a, w_i, b_i, lru_lambda)

    np_ = bp * tp
    tabs_p = _rope_tables(jnp.arange(tp, dtype=I32))
    tm = 256
    (xr, gr, q, k_p, v_p, iq, ik_p, ikb, iw, ga_p, gb_p) = _project(
        xp.reshape(np_, d), w_packed, tabs_p, tp // tm, tm)
    y_rnn_p, h_p = _rglru_prompt(xr, gr, bp, tp, lru_w, 256)
    conv_p = xr.reshape(bp, tp, d_rnn)[:, tp - (CONV_W - 1):, :]
    y_att_p = _attn_prompt(q, iq, iw, k_p, v_p, ikb, bp, tp, 256, 512)

    ns = bs * ts
    pos_s = past + jnp.arange(ts, dtype=I32)
    tabs_s = tuple(jnp.tile(t_, (ns // ts, 1)) for t_ in _rope_tables(pos_s))
    (xr_s, gr_s, q_s, k_s, v_s, iq_s, ik_s, ikb_s, iw_s, ga_s, gb_s) = _project(
        xs.reshape(ns, d), w_packed, tabs_s, ns // tm, tm)
    xp_s = jnp.concatenate([state_conv.astype(F32), xr_s.reshape(bs, ts, d_rnn)], axis=1)
    y_rnn_s, hs_s = _rglru_sample(xp_s, gr_s.reshape(bs, ts, d_rnn), state_h.reshape(bs, 1, d_rnn), lru_w, 16)
    conv_s = xp_s[:, ts:, :]
    h_s = hs_s[:, ts - 1, :]

    n_pool = cache_k.shape[0]
    iw_col = iw_s.reshape(bs, ts, IDX_HEADS).transpose(0, 2, 1).reshape(bs, IDX_HEADS * ts, 1)
    key_minor = lambda c: jnp.transpose(c, (0, 2, 3, 1)).reshape(n_pool, dkv, PAGE_SIZE)
    y_att_s = _attn_sample(q_s, iq_s, iw_col, k_s, v_s, ikb_s, key_minor(cache_k), key_minor(cache_v),
                           jnp.transpose(cache_idx_k, (0, 2, 1)), page_table, ts)

    wbr = w_branch.astype(BF16)
    h1 = _merge((y_rnn_p, y_att_p, ga_p, gb_p, xp.reshape(np_, d)),
                (y_rnn_s.reshape(ns, d_rnn), y_att_s, ga_s, gb_s, xs.reshape(ns, d)),
                wbr[:d_rnn], wbr[d_rnn:], w_out.astype(BF16),
                ln1_g.reshape(1, d), ln1_b.reshape(1, d), alpha, 256)

    w_r = w_router.astype(F32)
    w_r_hi = w_r.astype(BF16)
    w_r_lo = (w_r - w_r_hi.astype(F32)).astype(BF16)
    ne = w_router.shape[1]
    y = _moe(h1, w_r_hi, w_r_lo, router_bias.reshape(1, ne).astype(F32), w_exp_gate, w_exp_up, w_exp_down,
             w_sh_gate.astype(BF16), w_sh_up.astype(BF16), w_sh_down.astype(BF16),
             ln2_g.reshape(1, d), ln2_b.reshape(1, d), alpha)

    yp = y[:np_].reshape(bp, tp, d)
    ys = y[np_:].reshape(bs, ts, d)
    st = (k_p.reshape(bp, tp, N_KV_HEADS, HEAD_DIM), v_p.reshape(bp, tp, N_KV_HEADS, HEAD_DIM),
          ik_p.reshape(bp, tp, IDX_DIM), conv_p, h_p,
          k_s.reshape(bs, ts, N_KV_HEADS, HEAD_DIM), v_s.reshape(bs, ts, N_KV_HEADS, HEAD_DIM),
          ik_s.reshape(bs, ts, IDX_DIM), conv_s, h_s)
    return yp, ys, st


def kernel(x_prompt, x_sample, cache_k, cache_v, cache_idx_k, state_conv, state_h, page_table, w_in, conv_w,
           conv_b, w_a, b_a, w_i, b_i, lru_lambda, w_branch, w_out, ln1_g, ln1_b, w_router, router_bias,
           w_exp_gate, w_exp_up, w_exp_down, w_sh_gate, w_sh_up, w_sh_down, ln2_g, ln2_b):
    depth = w_in.shape[0]
    yp, ys = x_prompt, x_sample
    states = []
    for l in range(depth):
        yp, ys, st = _layer(
            yp, ys, cache_k[l], cache_v[l], cache_idx_k[l], state_conv[l], state_h[l], page_table, depth,
            w_in[l], conv_w[l], conv_b[l], w_a[l], b_a[l], w_i[l], b_i[l], lru_lambda[l], w_branch[l], w_out[l],
            ln1_g[l], ln1_b[l], w_router[l], router_bias[l], w_exp_gate[l], w_exp_up[l], w_exp_down[l],
            w_sh_gate[l], w_sh_up[l], w_sh_down[l], ln2_g[l], ln2_b[l])
        states.append(st)
    stacked = [jnp.stack(c) for c in zip(*states)]
    return (yp, ys, *stacked)
```
